```python
import math
import jax, jax.numpy as jnp
from jax import lax
import numpy as np

D_MODEL = 1024
BATCH = 4
SEQ = 8192
DEPTH = 2

N_META = 16
D_ATTN = 512
D_REC = 512
D_MIX = D_ATTN + D_REC
ATTN_HEADS = 4
ATTN_HEAD_DIM = 64
V_HEAD_DIM = 2 * ATTN_HEAD_DIM
ROPE_DIM = ATTN_HEAD_DIM // 4
ROPE_THETA = 500000.0
REC_BLOCKS = 8
REC_BLOCK_DIM = D_REC // REC_BLOCKS
CONV_WIDTH = 4
LRU_C = 8.0
D_FF = -(-8 * D_MODEL // (3 * 256)) * 256
Q_BLOCK = 128
EPS = 1e-6
N_IN = 3 * D_ATTN + 2 * D_REC

kernel_name = "hymba_diffattn_rglru_hybrid"


def rms_norm(x, g):
    xf = x.astype(jnp.float32)
    y = xf * lax.rsqrt(jnp.mean(xf * xf, axis=-1, keepdims=True) + EPS)
    return (y * g.astype(jnp.float32)).astype(x.dtype)


def rope_tables(T):
    inv = ROPE_THETA ** (-jnp.arange(0, ROPE_DIM, 2, dtype=jnp.float32) / ROPE_DIM)
    ang = jnp.arange(T, dtype=jnp.float32)[:, None] * inv[None, :]
    return jnp.cos(ang), jnp.sin(ang)


def apply_partial_rope(x, cos, sin):
    half = ROPE_DIM // 2
    c = cos[None, :, None, None, :].astype(x.dtype)
    s = sin[None, :, None, None, :].astype(x.dtype)
    x1 = x[..., :half]
    x2 = x[..., half:ROPE_DIM]
    return jnp.concatenate([x1 * c - x2 * s, x2 * c + x1 * s, x[..., ROPE_DIM:]], axis=-1)


def diff_attention(q, k, v, lam):
    B, T, H = q.shape[:3]
    nb = -(-T // Q_BLOCK)
    Tp = nb * Q_BLOCK
    padw = ((0, 0), (0, Tp - T), (0, 0), (0, 0), (0, 0))
    q = jnp.pad(q, padw)
    k = jnp.pad(k, padw)
    v = jnp.pad(v, padw[:4])
    qb = q.reshape(B, nb, Q_BLOCK, H, 2, ATTN_HEAD_DIM).transpose(1, 0, 2, 3, 4, 5)
    starts = jnp.arange(nb) * Q_BLOCK
    kpos = jnp.arange(Tp)
    scale = ATTN_HEAD_DIM ** -0.5

    def one_block(args):
        qblk, start = args
        s = jnp.einsum('bqhcd,bkhcd->bhcqk', qblk, k,
                       preferred_element_type=jnp.float32) * scale
        qpos = start + jnp.arange(Q_BLOCK)
        s = jnp.where(qpos[:, None] >= kpos[None, :], s, -jnp.inf)
        p = jax.nn.softmax(s, axis=-1)
        a = p[:, :, 0] - lam * p[:, :, 1]
        return jnp.einsum('bhqk,bkhe->bqhe', a.astype(v.dtype), v)

    o = lax.map(one_block, (qb, starts))
    return o.transpose(1, 0, 2, 3, 4).reshape(B, Tp, H, V_HEAD_DIM)[:, :T]


def rg_lru_branch(xr, gate, conv_w, conv_b, w_rg, b_rg, w_ig, b_ig, lru_L):
    B, T, _ = xr.shape
    xc = lax.conv_general_dilated(
        xr, conv_w[:, None, :].astype(xr.dtype), window_strides=(1,),
        padding=[(CONV_WIDTH - 1, 0)], dimension_numbers=('NWC', 'WIO', 'NWC'),
        feature_group_count=D_REC) + conv_b.astype(xr.dtype)
    xb = xc.reshape(B, T, REC_BLOCKS, REC_BLOCK_DIM)
    r = jax.nn.sigmoid((jnp.einsum('btnd,nde->btne', xb, w_rg).reshape(B, T, D_REC)
                        + b_rg).astype(jnp.float32))
    i = jax.nn.sigmoid((jnp.einsum('btnd,nde->btne', xb, w_ig).reshape(B, T, D_REC)
                        + b_ig).astype(jnp.float32))
    log_a = LRU_C * r * jax.nn.log_sigmoid(lru_L.astype(jnp.float32))
    a = jnp.exp(log_a)
    mult = jnp.sqrt(jnp.maximum(-jnp.expm1(2.0 * log_a), 0.0))
    mult = jnp.where(jnp.arange(T)[None, :, None] == 0, 1.0, mult)
    u = mult * i * xc.astype(jnp.float32)

    def combine(left, right):
        a1, b1 = left
        a2, b2 = right
        return a1 * a2, a2 * b1 + b2

    _, h = lax.associative_scan(combine, (a, u), axis=1)
    y = h * jax.nn.gelu(gate.astype(jnp.float32))
    return y.astype(xr.dtype)


def setup_inputs(seed: int = 0) -> dict:
    key = jax.random.key(seed)
    ks = jax.random.split(key, 24)
    f32 = jnp.float32
    L = DEPTH

    def nrm(k, shape, scale):
        return jax.random.normal(k, shape, f32) * scale

    u = jax.random.uniform(ks[15], (L, D_REC), f32, 0.9, 0.999)
    s = u ** (1.0 / LRU_C)
    lru_L = jnp.log(s) - jnp.log1p(-s)
    return {
        "x": nrm(ks[0], (BATCH, SEQ, D_MODEL), 1.0),
        "meta_tokens": nrm(ks[1], (N_META, D_MODEL), 1.0),
        "norm_mix_g": 1.0 + nrm(ks[2], (L, D_MODEL), 0.02),
        "w_in": nrm(ks[3], (L, D_MODEL, N_IN), D_MODEL ** -0.5),
        "q_norm_g": 1.0 + nrm(ks[4], (L, ATTN_HEAD_DIM), 0.02),
        "k_norm_g": 1.0 + nrm(ks[5], (L, ATTN_HEAD_DIM), 0.02),
        "lambda_q1": nrm(ks[6], (L, ATTN_HEAD_DIM), 0.1),
        "lambda_k1": nrm(ks[7], (L, ATTN_HEAD_DIM), 0.1),
        "lambda_q2": nrm(ks[8], (L, ATTN_HEAD_DIM), 0.1),
        "lambda_k2": nrm(ks[9], (L, ATTN_HEAD_DIM), 0.1),
        "subln_g": 1.0 + nrm(ks[10], (L, V_HEAD_DIM), 0.02),
        "conv_w": nrm(ks[11], (L, CONV_WIDTH, D_REC), CONV_WIDTH ** -0.5),
        "conv_b": nrm(ks[12], (L, D_REC), 0.01),
        "w_rg": nrm(ks[13], (L, REC_BLOCKS, REC_BLOCK_DIM, REC_BLOCK_DIM), REC_BLOCK_DIM ** -0.5),
        "b_rg": nrm(ks[14], (L, D_REC), 0.01),
        "w_ig": nrm(ks[16], (L, REC_BLOCKS, REC_BLOCK_DIM, REC_BLOCK_DIM), REC_BLOCK_DIM ** -0.5),
        "b_ig": nrm(ks[17], (L, D_REC), 0.01),
        "lru_L": lru_L,
        "rec_norm_g": 1.0 + nrm(ks[18], (L, D_REC), 0.02),
        "w_out": nrm(ks[19], (L, D_MIX, D_MODEL), D_MIX ** -0.5),
        "norm_ffn_g": 1.0 + nrm(ks[20], (L, D_MODEL), 0.02),
        "w_gu": nrm(ks[21], (L, D_MODEL, 2 * D_FF), D_MODEL ** -0.5),
        "w_down": nrm(ks[22], (L, D_FF, D_MODEL), D_FF ** -0.5),
    }


def reference(x, meta_tokens, norm_mix_g, w_in, q_norm_g, k_norm_g, lambda_q1, lambda_k1,
              lambda_q2, lambda_k2, subln_g, conv_w, conv_b, w_rg, b_rg, w_ig, b_ig, lru_L,
              rec_norm_g, w_out, norm_ffn_g, w_gu, w_down):
    B = x.shape[0]
    meta = jnp.broadcast_to(meta_tokens.astype(x.dtype)[None], (B, N_META, D_MODEL))
    h = jnp.concatenate([meta, x], axis=1)
    T = h.shape[1]
    cos, sin = rope_tables(T)
    splits = [D_ATTN, 2 * D_ATTN, 3 * D_ATTN, 3 * D_ATTN + D_REC]

    for l in range(DEPTH):
        lam_init = 0.8 - 0.6 * math.exp(-0.3 * l)
        hn = rms_norm(h, norm_mix_g[l])
        proj = hn @ w_in[l]
        q, k, v, xr, gate = jnp.split(proj, splits, axis=-1)
        q = q.reshape(B, T, ATTN_HEADS, 2, ATTN_HEAD_DIM)
        k = k.reshape(B, T, ATTN_HEADS, 2, ATTN_HEAD_DIM)
        v = v.reshape(B, T, ATTN_HEADS, V_HEAD_DIM)
        q = apply_partial_rope(rms_norm(q, q_norm_g[l]), cos, sin)
        k = apply_partial_rope(rms_norm(k, k_norm_g[l]), cos, sin)
        lam = (jnp.exp(jnp.sum(lambda_q1[l].astype(jnp.float32) * lambda_k1[l].astype(jnp.float32)))
               - jnp.exp(jnp.sum(lambda_q2[l].astype(jnp.float32) * lambda_k2[l].astype(jnp.float32)))
               + lam_init)
        o_attn = diff_attention(q, k, v, lam)
        o_attn = (rms_norm(o_attn, subln_g[l]) * (1.0 - lam_init)).reshape(B, T, D_ATTN)
        o_rec = rg_lru_branch(xr, gate, conv_w[l], conv_b[l], w_rg[l], b_rg[l],
                              w_ig[l], b_ig[l], lru_L[l])
        o_rec = rms_norm(o_rec, rec_norm_g[l])
        h = h + jnp.concatenate([o_attn, o_rec], axis=-1) @ w_out[l]
        hn = rms_norm(h, norm_ffn_g[l])
        g, u = jnp.split(hn @ w_gu[l], [D_FF], axis=-1)
        h = h + (jax.nn.silu(g) * u) @ w_down[l]

    return h[:, N_META:, :]
```

```python
import functools
import math

import jax
import jax.numpy as jnp
from jax import lax
from jax.experimental import pallas as pl
from jax.experimental.pallas import tpu as pltpu

D_MODEL = 1024
DEPTH = 2
N_META = 16
D_ATTN = 512
D_REC = 512
ATTN_HEADS = 4
ATTN_HEAD_DIM = 64
V_HEAD_DIM = 2 * ATTN_HEAD_DIM
ROPE_DIM = ATTN_HEAD_DIM // 4
ROPE_THETA = 500000.0
REC_BLOCKS = 8
REC_BLOCK_DIM = D_REC // REC_BLOCKS
CONV_WIDTH = 4
LRU_C = 8.0
D_FF = 2816
EPS = 1e-6
N_IN = 3 * D_ATTN + 2 * D_REC

LANE = 128
SUBLANE = 8
BF16_SUBLANE = 16
FRONT_PAD = LANE - N_META
MAIN0 = LANE
V_ROWS = V_HEAD_DIM + BF16_SUBLANE
NEG = -1e30
VMEM_LIMIT = 56 * 1024 * 1024

ROW_TILE = 640
FFN_ROW_TILE = 512
FFN_CHUNK = 1408
ATTN_TILE = 512

f32 = jnp.float32
bf16 = jnp.bfloat16


def _const_spec(shape):
    zeros = (0,) * len(shape)
    return pl.BlockSpec(shape, lambda *_: zeros, pipeline_mode=pl.Buffered(1))


def _qk_norm_rope(z, gain, gmean_ref, cos, sin_a, sin_b, scale):
    ms = jnp.dot((z * z).astype(bf16), gmean_ref[...], preferred_element_type=f32)
    zn = z * lax.rsqrt(ms + EPS) * gain
    if scale != 1.0:
        zn = zn * scale
    outs = []
    for c in range(D_ATTN // LANE):
        zc = zn[:, c * LANE:(c + 1) * LANE]
        half = ROPE_DIM // 2
        outs.append(zc * cos
                    + pltpu.roll(zc, LANE - half, 1) * sin_a
                    + pltpu.roll(zc, half, 1) * sin_b)
    return jnp.concatenate(outs, axis=1)


def _in_proj_kernel(h_ref, g_ref, w_ref, qg_ref, kg_ref, cos_ref, sa_ref, sb_ref, gmean_ref,
                    qT_ref, k_ref, vT_ref, xr_ref, gate_ref, *, tm):
    j = pl.program_id(1)
    x = h_ref[...]
    row = j * tm + lax.broadcasted_iota(jnp.int32, (tm, 1), 0)
    ms = jnp.mean(x * x, axis=-1, keepdims=True)
    hn = x * lax.rsqrt(ms + EPS) * g_ref[...]
    hn = jnp.where(row >= FRONT_PAD, hn, 0.0).astype(bf16)

    def proj(c0, width):
        return jnp.dot(hn, w_ref[:, c0:c0 + width], preferred_element_type=f32)

    cos, sa, sb = cos_ref[...], sa_ref[...], sb_ref[...]
    q = _qk_norm_rope(proj(0, D_ATTN), qg_ref[...], gmean_ref, cos, sa, sb,
                      ATTN_HEAD_DIM ** -0.5)
    qT_ref[...] = q.T.astype(bf16)
    k = _qk_norm_rope(proj(D_ATTN, D_ATTN), kg_ref[...], gmean_ref, cos, sa, sb, 1.0)
    k_ref[...] = k.astype(bf16)
    vT = proj(2 * D_ATTN, D_ATTN).T.astype(bf16)
    ones = jnp.ones((BF16_SUBLANE, tm), bf16)
    for hd in range(ATTN_HEADS):
        vT_ref[hd, 0:V_HEAD_DIM, :] = vT[hd * V_HEAD_DIM:(hd + 1) * V_HEAD_DIM, :]
        vT_ref[hd, V_HEAD_DIM:V_ROWS, :] = ones
    xr_ref[...] = proj(3 * D_ATTN, D_REC)
    gate_ref[...] = proj(3 * D_ATTN + D_REC, D_REC)


def _in_proj(h, g, w, qg, kg, cos_t, sa_t, sb_t, gmean):
    B, T, D = h.shape
    tm = ROW_TILE
    nt = T // tm
    row_blk = lambda width: pl.BlockSpec((None, tm, width), lambda b, j: (b, j, 0))
    tab_blk = pl.BlockSpec((tm, LANE), lambda b, j: (j, 0))
    return pl.pallas_call(
        functools.partial(_in_proj_kernel, tm=tm),
        grid=(B, nt),
        in_specs=[row_blk(D), _const_spec((1, D)), _const_spec((D, N_IN)),
                  _const_spec((1, D_ATTN)), _const_spec((1, D_ATTN)),
                  tab_blk, tab_blk, tab_blk, _const_spec((D_ATTN, D_ATTN))],
        out_specs=[pl.BlockSpec((None, D_ATTN, tm), lambda b, j: (b, 0, j)),
                   row_blk(D_ATTN),
                   pl.BlockSpec((None, ATTN_HEADS, V_ROWS, tm), lambda b, j: (b, 0, 0, j)),
                   row_blk(D_REC), row_blk(D_REC)],
        out_shape=[jax.ShapeDtypeStruct((B, D_ATTN, T), bf16),
                   jax.ShapeDtypeStruct((B, T, D_ATTN), bf16),
                   jax.ShapeDtypeStruct((B, ATTN_HEADS, V_ROWS, T), bf16),
                   jax.ShapeDtypeStruct((B, T, D_REC), f32),
                   jax.ShapeDtypeStruct((B, T, D_REC), f32)],
        compiler_params=pltpu.CompilerParams(
            dimension_semantics=("parallel", "parallel"), vmem_limit_bytes=VMEM_LIMIT),
        name="in_proj",
    )(h, g, w, qg, kg, cos_t, sa_t, sb_t, gmean)


def _attn_kernel(qT_ref, k_ref, vT_ref, lq1_ref, lk1_ref, lq2_ref, lk2_ref, g_ref, o_ref,
                 q2_ref, acc_ref, m_ref, *, tile, n_main, lam_init):
    lam = (jnp.exp(jnp.sum(lq1_ref[...] * lk1_ref[...], keepdims=True))
           - jnp.exp(jnp.sum(lq2_ref[...] * lk2_ref[...], keepdims=True)) + lam_init)
    hd = ATTN_HEAD_DIM

    def start_tile(qs, w):
        qt = qT_ref[:, pl.ds(qs, w)]
        zero = jnp.zeros((hd, w), bf16)
        q2_ref[0:hd, 0:w] = qt[0:hd]
        q2_ref[hd:2 * hd, 0:w] = zero
        q2_ref[0:hd, w:2 * w] = zero
        q2_ref[hd:2 * hd, w:2 * w] = qt[hd:2 * hd]
        m_ref[:, 0:2 * w] = jnp.full((1, 2 * w), NEG, f32)
        acc_ref[:, 0:2 * w] = jnp.zeros((V_ROWS, 2 * w), f32)

    def chunk(ks, kw, qs, w, skip_pad, causal):
        kc = k_ref[pl.ds(ks, kw), :]
        s = jnp.dot(kc, q2_ref[:, 0:2 * w], preferred_element_type=f32)
        if skip_pad or causal:
            kidx = ks + lax.broadcasted_iota(jnp.int32, (kw, 2 * w), 0)
            ok = None
            if skip_pad:
                ok = kidx >= FRONT_PAD
            if causal:
                col = lax.broadcasted_iota(jnp.int32, (kw, 2 * w), 1)
                qidx = qs + jnp.where(col >= w, col - w, col)
                c_ok = kidx <= qidx
                ok = c_ok if ok is None else ok & c_ok
            s = jnp.where(ok, s, NEG)
        m_prev = m_ref[:, 0:2 * w]
        m_new = jnp.maximum(m_prev, jnp.max(s, axis=0, keepdims=True))
        alpha = jnp.exp(m_prev - m_new)
        p = jnp.exp(s - m_new).astype(bf16)
        vc = vT_ref[:, pl.ds(ks, kw)]
        acc_ref[:, 0:2 * w] = (alpha * acc_ref[:, 0:2 * w]
                               + jnp.dot(vc, p, preferred_element_type=f32))
        m_ref[:, 0:2 * w] = m_new

    def finish_tile(qs, w):
        acc = acc_ref[:, 0:2 * w]
        denom = acc[V_HEAD_DIM:V_HEAD_DIM + 1, :]
        on = acc[0:V_HEAD_DIM, :] * (1.0 / denom)
        o = on[:, 0:w] - lam * on[:, w:2 * w]
        ms = jnp.mean(o * o, axis=0, keepdims=True)
        y = o * lax.rsqrt(ms + EPS) * g_ref[...] * (1.0 - lam_init)
        o_ref[pl.ds(qs, w), :] = y.T.astype(bf16)

    start_tile(0, LANE)
    chunk(0, LANE, 0, LANE, True, True)
    finish_tile(0, LANE)

    def q_tile(qi, carry):
        qs = pl.multiple_of(MAIN0 + qi * tile, LANE)
        start_tile(qs, tile)
        chunk(0, LANE, qs, tile, True, False)

        def kv(jj, c):
            chunk(pl.multiple_of(MAIN0 + jj * tile, LANE), tile, qs, tile, False, False)
            return c

        lax.fori_loop(0, qi, kv, 0)
        chunk(qs, tile, qs, tile, False, True)
        finish_tile(qs, tile)
        return carry

    lax.fori_loop(0, n_main, q_tile, 0)


def _attention(qT, k, vT, lq1, lk1, lq2, lk2, g_col, lam_init):
    B, _, T = qT.shape
    tile = ATTN_TILE
    n_main = (T - MAIN0) // tile
    vec = _const_spec((1, ATTN_HEAD_DIM))
    return pl.pallas_call(
        functools.partial(_attn_kernel, tile=tile, n_main=n_main, lam_init=lam_init),
        grid=(B, ATTN_HEADS),
        in_specs=[pl.BlockSpec((None, V_HEAD_DIM, T), lambda b, h: (b, h, 0)),
                  pl.BlockSpec((None, T, V_HEAD_DIM), lambda b, h: (b, 0, h)),
                  pl.BlockSpec((None, None, V_ROWS, T), lambda b, h: (b, h, 0, 0)),
                  vec, vec, vec, vec, _const_spec((V_HEAD_DIM, 1))],
        out_specs=pl.BlockSpec((None, T, V_HEAD_DIM), lambda b, h: (b, 0, h)),
        out_shape=jax.ShapeDtypeStruct((B, T, D_ATTN), bf16),
        scratch_shapes=[pltpu.VMEM((V_HEAD_DIM, 2 * tile), bf16),
                        pltpu.VMEM((V_ROWS, 2 * tile), f32),
                        pltpu.VMEM((1, 2 * tile), f32)],
        compiler_params=pltpu.CompilerParams(
            dimension_semantics=("parallel", "parallel"), vmem_limit_bytes=VMEM_LIMIT),
        name="diff_attn",
    )(qT, k, vT, lq1, lk1, lq2, lk2, g_col)


def _rglru_kernel(xr_ref, gate_ref, cw_ref, cb_ref, wr_ref, wi_ref, brg_ref, big_ref, L_ref,
                  g_ref, o_ref, xbuf_ref, h_ref, *, tc):
    j = pl.program_id(1)

    @pl.when(j == 0)
    def _():
        xbuf_ref[0:SUBLANE, :] = jnp.zeros((SUBLANE, D_REC), f32)
        h_ref[...] = jnp.zeros((1, D_REC), f32)

    x = xr_ref[...]
    xbuf_ref[SUBLANE:SUBLANE + tc, :] = x
    xc = cb_ref[...]
    for t in range(CONV_WIDTH):
        off = SUBLANE - (CONV_WIDTH - 1) + t
        xc = xc + cw_ref[t:t + 1, :] * xbuf_ref[off:off + tc, :]
    xbuf_ref[0:SUBLANE, :] = x[tc - SUBLANE:tc, :]

    xcb = xc.astype(bf16)
    half = D_REC // 2

    def gate_lin(w_ref, b_ref):
        lo = jnp.dot(xcb[:, 0:half], w_ref[0], preferred_element_type=f32)
        hi = jnp.dot(xcb[:, half:D_REC], w_ref[1], preferred_element_type=f32)
        return jnp.concatenate([lo, hi], axis=1) + b_ref[...]

    r = jax.nn.sigmoid(gate_lin(wr_ref, brg_ref))
    i = jax.nn.sigmoid(gate_lin(wi_ref, big_ref))
    log_a = LRU_C * r * jax.nn.log_sigmoid(L_ref[...])
    a = jnp.exp(log_a)
    mult = jnp.sqrt(jnp.maximum(-jnp.tanh(log_a) * (a * a + 1.0), 0.0))
    row = lax.broadcasted_iota(jnp.int32, (tc, 1), 0)
    t_idx = j * tc + row
    mult = jnp.where(t_idx == FRONT_PAD, 1.0, mult)
    u = jnp.where(t_idx >= FRONT_PAD, mult * i * xc, 0.0)

    d = 1
    while d < tc:
        a_s = jnp.where(row >= d, pltpu.roll(a, d, 0), 1.0)
        u_s = jnp.where(row >= d, pltpu.roll(u, d, 0), 0.0)
        u = a * u_s + u
        a = a * a_s
        d *= 2
    hseq = u + a * h_ref[...]
    h_ref[...] = hseq[tc - 1:tc, :]

    y = hseq * jax.nn.gelu(gate_ref[...])
    ms = jnp.mean(y * y, axis=-1, keepdims=True)
    o_ref[...] = (y * lax.rsqrt(ms + EPS) * g_ref[...]).astype(bf16)


def _rglru(xr, gate, cw, cb, wr, wi, brg, big, L, g):
    B, T, _ = xr.shape
    tc = ROW_TILE
    row_blk = pl.BlockSpec((None, tc, D_REC), lambda b, j: (b, j, 0))
    vec = _const_spec((1, D_REC))
    half = D_REC // 2
    return pl.pallas_call(
        functools.partial(_rglru_kernel, tc=tc),
        grid=(B, T // tc),
        in_specs=[row_blk, row_blk, _const_spec((CONV_WIDTH, D_REC)), vec,
                  _const_spec((2, half, half)), _const_spec((2, half, half)),
                  vec, vec, vec, vec],
        out_specs=row_blk,
        out_shape=jax.ShapeDtypeStruct((B, T, D_REC), bf16),
        scratch_shapes=[pltpu.VMEM((SUBLANE + tc, D_REC), f32), pltpu.VMEM((1, D_REC), f32)],
        compiler_params=pltpu.CompilerParams(
            dimension_semantics=("parallel", "arbitrary"), vmem_limit_bytes=VMEM_LIMIT),
        name="rglru",
    )(xr, gate, cw, cb, wr, wi, brg, big, L, g)


def _out_ffn_kernel(h_ref, oa_ref, or_ref, wo_ref, g_ref, wgu_ref, wd_ref, out_ref):
    out_ref[...] = (h_ref[...]
                    + jnp.dot(oa_ref[...], wo_ref[0:D_ATTN, :], preferred_element_type=f32)
                    + jnp.dot(or_ref[...], wo_ref[D_ATTN:D_ATTN + D_REC, :],
                              preferred_element_type=f32))
    h1 = out_ref[...]
    ms = jnp.mean(h1 * h1, axis=-1, keepdims=True)
    hn = (h1 * lax.rsqrt(ms + EPS) * g_ref[...]).astype(bf16)
    ffn = None
    for c0 in range(0, D_FF, FFN_CHUNK):
        gt = jnp.dot(hn, wgu_ref[:, c0:c0 + FFN_CHUNK], preferred_element_type=f32)
        up = jnp.dot(hn, wgu_ref[:, D_FF + c0:D_FF + c0 + FFN_CHUNK],
                     preferred_element_type=f32)
        act = (jax.nn.silu(gt) * up).astype(bf16)
        part = jnp.dot(act, wd_ref[c0:c0 + FFN_CHUNK, :], preferred_element_type=f32)
        ffn = part if ffn is None else ffn + part
    out_ref[...] += ffn


def _out_ffn(h, oa, orec, wo, g, wgu, wd):
    R, D = h.shape
    tm = FFN_ROW_TILE
    row_blk = lambda width: pl.BlockSpec((tm, width), lambda i: (i, 0))
    return pl.pallas_call(
        _out_ffn_kernel,
        grid=(R // tm,),
        in_specs=[row_blk(D), row_blk(D_ATTN), row_blk(D_REC),
                  _const_spec((D_ATTN + D_REC, D)), _const_spec((1, D)),
                  _const_spec((D, 2 * D_FF)), _const_spec((D_FF, D))],
        out_specs=row_blk(D),
        out_shape=jax.ShapeDtypeStruct((R, D), f32),
        compiler_params=pltpu.CompilerParams(
            dimension_semantics=("parallel",), vmem_limit_bytes=VMEM_LIMIT),
        name="out_ffn",
    )(h, oa, orec, wo, g, wgu, wd)


def _rope_tables(T):
    half = ROPE_DIM // 2
    inv = ROPE_THETA ** (-jnp.arange(0, ROPE_DIM, 2, dtype=f32) / ROPE_DIM)
    pos = (jnp.arange(T) - FRONT_PAD).astype(f32)
    ang = pos[:, None] * inv[None, :]
    d = jnp.arange(LANE) % ATTN_HEAD_DIM
    first, second = d < half, (d >= half) & (d < ROPE_DIM)
    cos_l = jnp.cos(ang)[:, d % half]
    sin_l = jnp.sin(ang)[:, d % half]
    cos_t = jnp.where(first | second, cos_l, 1.0)
    sin_a = jnp.where(first, -sin_l, 0.0)
    sin_b = jnp.where(second, sin_l, 0.0)
    return cos_t, sin_a, sin_b


def _block_diag_halves(w):
    nb = REC_BLOCKS // 2
    w4 = w.reshape(2, nb, REC_BLOCK_DIM, REC_BLOCK_DIM)
    eye = jnp.eye(nb, dtype=w.dtype)
    half = D_REC // 2
    return jnp.einsum('hnde,nm->hndme', w4, eye).reshape(2, half, half)


def kernel(x, meta_tokens, norm_mix_g, w_in, q_norm_g, k_norm_g, lambda_q1, lambda_k1,
           lambda_q2, lambda_k2, subln_g, conv_w, conv_b, w_rg, b_rg, w_ig, b_ig, lru_L,
           rec_norm_g, w_out, norm_ffn_g, w_gu, w_down):
    B, S, D = x.shape
    assert D == D_MODEL and S % ATTN_TILE == 0
    T = S + LANE
    assert T % ROW_TILE == 0 and (B * T) % FFN_ROW_TILE == 0

    meta = jnp.broadcast_to(meta_tokens.astype(x.dtype)[None], (B, N_META, D))
    h = jnp.concatenate([jnp.zeros((B, FRONT_PAD, D), x.dtype), meta, x], axis=1)
    cos_t, sin_a, sin_b = _rope_tables(T)
    grp = jnp.arange(D_ATTN) // ATTN_HEAD_DIM
    gmean = jnp.where(grp[:, None] == grp[None, :], 1.0 / ATTN_HEAD_DIM, 0.0).astype(bf16)
    n_grp = D_ATTN // ATTN_HEAD_DIM
    row = lambda v: v.reshape(1, -1)

    for l in range(DEPTH):
        lam_init = 0.8 - 0.6 * math.exp(-0.3 * l)
        qT, k, vT, xr, gate = _in_proj(
            h, row(norm_mix_g[l]), w_in[l].astype(bf16),
            row(jnp.tile(q_norm_g[l], n_grp)), row(jnp.tile(k_norm_g[l], n_grp)),
            cos_t, sin_a, sin_b, gmean)
        o_attn = _attention(qT, k, vT, row(lambda_q1[l]), row(lambda_k1[l]),
                            row(lambda_q2[l]), row(lambda_k2[l]),
                            subln_g[l].reshape(V_HEAD_DIM, 1), lam_init)
        o_rec = _rglru(xr, gate, conv_w[l], row(conv_b[l]),
                       _block_diag_halves(w_rg[l]).astype(bf16),
                       _block_diag_halves(w_ig[l]).astype(bf16),
                       row(b_rg[l]), row(b_ig[l]), row(lru_L[l]), row(rec_norm_g[l]))
        h = _out_ffn(h.reshape(B * T, D), o_attn.reshape(B * T, D_ATTN),
                     o_rec.reshape(B * T, D_REC), w_out[l].astype(bf16),
                     row(norm_ffn_g[l]), w_gu[l].astype(bf16),
                     w_down[l].astype(bf16)).reshape(B, T, D)

    return h[:, MAIN0:, :]
```

```python
import functools
import math

import jax
import jax.numpy as jnp
from jax import lax
from jax.experimental import pallas as pl
from jax.experimental.pallas import tpu as pltpu

D_MODEL = 1024
DEPTH = 2
N_META = 16
D_ATTN = 512
D_REC = 512
ATTN_HEADS = 4
ATTN_HEAD_DIM = 64
V_HEAD_DIM = 2 * ATTN_HEAD_DIM
ROPE_DIM = ATTN_HEAD_DIM // 4
ROPE_THETA = 500000.0
REC_BLOCKS = 8
REC_BLOCK_DIM = D_REC // REC_BLOCKS
CONV_WIDTH = 4
LRU_C = 8.0
D_FF = 2816
EPS = 1e-6
N_IN = 3 * D_ATTN + 2 * D_REC

LANE = 128
SUBLANE = 8
BF16_SUBLANE = 16
FRONT_PAD = LANE - N_META
MAIN0 = LANE
V_ROWS = V_HEAD_DIM + BF16_SUBLANE
NEG = -1e30
VMEM_LIMIT = 56 * 1024 * 1024

ROW_TILE = 640
FFN_ROW_TILE = 512
FFN_CHUNK = 1408
ATTN_TILE = 512
ATTN_COL_BLOCK = 256

f32 = jnp.float32
bf16 = jnp.bfloat16


def _const_spec(shape):
    zeros = (0,) * len(shape)
    return pl.BlockSpec(shape, lambda *_: zeros, pipeline_mode=pl.Buffered(1))


def _qk_norm_rope(z, gain, gmean_ref, cos, sin_a, sin_b, scale):
    ms = jnp.dot((z * z).astype(bf16), gmean_ref[...], preferred_element_type=f32)
    zn = z * lax.rsqrt(ms + EPS) * gain
    if scale != 1.0:
        zn = zn * scale
    outs = []
    for c in range(D_ATTN // LANE):
        zc = zn[:, c * LANE:(c + 1) * LANE]
        half = ROPE_DIM // 2
        outs.append(zc * cos
                    + pltpu.roll(zc, LANE - half, 1) * sin_a
                    + pltpu.roll(zc, half, 1) * sin_b)
    return jnp.concatenate(outs, axis=1)


def _in_proj_kernel(h_ref, g_ref, w_ref, qg_ref, kg_ref, cos_ref, sa_ref, sb_ref, gmean_ref,
                    qT_ref, k_ref, vT_ref, xr_ref, gate_ref, *, tm):
    j = pl.program_id(1)
    x = h_ref[...]
    row = j * tm + lax.broadcasted_iota(jnp.int32, (tm, 1), 0)
    ms = jnp.mean(x * x, axis=-1, keepdims=True)
    hn = x * lax.rsqrt(ms + EPS) * g_ref[...]
    hn = jnp.where(row >= FRONT_PAD, hn, 0.0).astype(bf16)

    def proj(c0, width):
        return jnp.dot(hn, w_ref[:, c0:c0 + width], preferred_element_type=f32)

    cos, sa, sb = cos_ref[...], sa_ref[...], sb_ref[...]
    q = _qk_norm_rope(proj(0, D_ATTN), qg_ref[...], gmean_ref, cos, sa, sb,
                      ATTN_HEAD_DIM ** -0.5 * math.log2(math.e))
    qT_ref[...] = q.T.astype(bf16)
    k = _qk_norm_rope(proj(D_ATTN, D_ATTN), kg_ref[...], gmean_ref, cos, sa, sb, 1.0)
    k_ref[...] = k.astype(bf16)
    vT = proj(2 * D_ATTN, D_ATTN).T.astype(bf16)
    ones = jnp.ones((BF16_SUBLANE, tm), bf16)
    for hd in range(ATTN_HEADS):
        vT_ref[hd, 0:V_HEAD_DIM, :] = vT[hd * V_HEAD_DIM:(hd + 1) * V_HEAD_DIM, :]
        vT_ref[hd, V_HEAD_DIM:V_ROWS, :] = ones
    xr_ref[...] = proj(3 * D_ATTN, D_REC)
    gate_ref[...] = proj(3 * D_ATTN + D_REC, D_REC)


def _in_proj(h, g, w, qg, kg, cos_t, sa_t, sb_t, gmean):
    B, T, D = h.shape
    tm = ROW_TILE
    nt = T // tm
    row_blk = lambda width: pl.BlockSpec((None, tm, width), lambda b, j: (b, j, 0))
    tab_blk = pl.BlockSpec((tm, LANE), lambda b, j: (j, 0))
    return pl.pallas_call(
        functools.partial(_in_proj_kernel, tm=tm),
        grid=(B, nt),
        in_specs=[row_blk(D), _const_spec((1, D)), _const_spec((D, N_IN)),
                  _const_spec((1, D_ATTN)), _const_spec((1, D_ATTN)),
                  tab_blk, tab_blk, tab_blk, _const_spec((D_ATTN, D_ATTN))],
        out_specs=[pl.BlockSpec((None, D_ATTN, tm), lambda b, j: (b, 0, j)),
                   row_blk(D_ATTN),
                   pl.BlockSpec((None, ATTN_HEADS, V_ROWS, tm), lambda b, j: (b, 0, 0, j)),
                   row_blk(D_REC), row_blk(D_REC)],
        out_shape=[jax.ShapeDtypeStruct((B, D_ATTN, T), bf16),
                   jax.ShapeDtypeStruct((B, T, D_ATTN), bf16),
                   jax.ShapeDtypeStruct((B, ATTN_HEADS, V_ROWS, T), bf16),
                   jax.ShapeDtypeStruct((B, T, D_REC), f32),
                   jax.ShapeDtypeStruct((B, T, D_REC), f32)],
        compiler_params=pltpu.CompilerParams(
            dimension_semantics=("parallel", "parallel"), vmem_limit_bytes=VMEM_LIMIT),
        name="in_proj",
    )(h, g, w, qg, kg, cos_t, sa_t, sb_t, gmean)


def _attn_kernel(qT_ref, k_ref, vT_ref, lq1_ref, lk1_ref, lq2_ref, lk2_ref, g_ref, o_ref,
                 q2_ref, acc_ref, m_ref, sa_ref, sb_ref, bias_ref, *, tile, n_main, lam_init):
    lam = (jnp.exp(jnp.sum(lq1_ref[...] * lk1_ref[...], keepdims=True))
           - jnp.exp(jnp.sum(lq2_ref[...] * lk2_ref[...], keepdims=True)) + lam_init)
    hd = ATTN_HEAD_DIM

    def load_q(qs, w):
        qt = qT_ref[:, pl.ds(qs, w)]
        zero = jnp.zeros((hd, w), bf16)
        q2_ref[0:hd, 0:w] = qt[0:hd]
        q2_ref[hd:2 * hd, 0:w] = zero
        q2_ref[0:hd, w:2 * w] = zero
        q2_ref[hd:2 * hd, w:2 * w] = qt[hd:2 * hd]

    def finish_tile(qs, w):
        acc = acc_ref[:, 0:2 * w]
        denom = acc[V_HEAD_DIM:V_HEAD_DIM + 1, :]
        on = acc[0:V_HEAD_DIM, :] * (1.0 / denom)
        o = on[:, 0:w] - lam * on[:, w:2 * w]
        ms = jnp.mean(o * o, axis=0, keepdims=True)
        y = o * lax.rsqrt(ms + EPS) * g_ref[...] * (1.0 - lam_init)
        o_ref[pl.ds(qs, w), :] = y.T.astype(bf16)

    w0 = LANE
    load_q(0, w0)
    s0 = jnp.dot(k_ref[0:w0, :], q2_ref[:, 0:2 * w0], preferred_element_type=f32)
    kidx = lax.broadcasted_iota(jnp.int32, (w0, 2 * w0), 0)
    col = lax.broadcasted_iota(jnp.int32, (w0, 2 * w0), 1)
    qidx = jnp.where(col >= w0, col - w0, col)
    s0 = jnp.where((kidx >= FRONT_PAD) & (kidx <= qidx), s0, NEG)
    p0 = jnp.exp2(s0 - jnp.max(s0, axis=0, keepdims=True)).astype(bf16)
    acc_ref[:, 0:2 * w0] = jnp.dot(vT_ref[:, 0:w0], p0, preferred_element_type=f32)
    finish_tile(0, w0)

    kk = lax.broadcasted_iota(jnp.int32, (tile, 2 * tile), 0)
    cc = lax.broadcasted_iota(jnp.int32, (tile, 2 * tile), 1)
    bias_ref[...] = jnp.where(kk <= jnp.where(cc >= tile, cc - tile, cc), 0.0, NEG)

    def meta_keys():
        sm = jnp.dot(k_ref[FRONT_PAD:MAIN0, :], q2_ref[...], preferred_element_type=f32)
        m0 = jnp.max(sm, axis=0, keepdims=True)
        pm = jnp.exp2(sm - m0).astype(bf16)
        pm = jnp.concatenate([jnp.zeros((FRONT_PAD, 2 * tile), bf16), pm], axis=0)
        acc_ref[...] = jnp.dot(vT_ref[:, 0:MAIN0], pm, preferred_element_type=f32)
        m_ref[...] = m0

    def scores(ks, s_ref):
        s_ref[...] = jnp.dot(k_ref[pl.ds(ks, tile), :], q2_ref[...],
                             preferred_element_type=f32)

    def softmax_pv(ks, s_ref, diagonal):
        vc = vT_ref[:, pl.ds(ks, tile)]
        for c0 in range(0, 2 * tile, ATTN_COL_BLOCK):
            cs = slice(c0, c0 + ATTN_COL_BLOCK)
            sc = s_ref[:, cs]
            if diagonal:
                sc = sc + bias_ref[:, cs]
            m_prev = m_ref[:, cs]
            m_new = jnp.maximum(m_prev, jnp.max(sc, axis=0, keepdims=True))
            alpha = jnp.exp2(m_prev - m_new)
            pc = jnp.exp2(sc - m_new).astype(bf16)
            acc_ref[:, cs] = alpha * acc_ref[:, cs] + jnp.dot(vc, pc, preferred_element_type=f32)
            m_ref[:, cs] = m_new

    def q_tile(qi, carry):
        qs = pl.multiple_of(MAIN0 + qi * tile, LANE)
        load_q(qs, tile)
        meta_keys()
        scores(MAIN0, sa_ref)

        def step(ks, cur_ref, next_ref):
            scores(pl.multiple_of(ks + tile, LANE), next_ref)
            softmax_pv(ks, cur_ref, False)

        def pair(t, c):
            ks = pl.multiple_of(MAIN0 + 2 * t * tile, LANE)
            step(ks, sa_ref, sb_ref)
            step(pl.multiple_of(ks + tile, LANE), sb_ref, sa_ref)
            return c

        lax.fori_loop(0, qi // 2, pair, 0)
        odd = qi % 2 == 1

        @pl.when(odd)
        def _():
            step(pl.multiple_of(qs - tile, LANE), sa_ref, sb_ref)
            softmax_pv(qs, sb_ref, True)

        @pl.when(jnp.logical_not(odd))
        def _():
            softmax_pv(qs, sa_ref, True)

        finish_tile(qs, tile)
        return carry

    lax.fori_loop(0, n_main, q_tile, 0)


def _attention(qT, k, vT, lq1, lk1, lq2, lk2, g_col, lam_init):
    B, _, T = qT.shape
    tile = ATTN_TILE
    n_main = (T - MAIN0) // tile
    vec = _const_spec((1, ATTN_HEAD_DIM))
    return pl.pallas_call(
        functools.partial(_attn_kernel, tile=tile, n_main=n_main, lam_init=lam_init),
        grid=(B, ATTN_HEADS),
        in_specs=[pl.BlockSpec((None, V_HEAD_DIM, T), lambda b, h: (b, h, 0)),
                  pl.BlockSpec((None, T, V_HEAD_DIM), lambda b, h: (b, 0, h)),
                  pl.BlockSpec((None, None, V_ROWS, T), lambda b, h: (b, h, 0, 0)),
                  vec, vec, vec, vec, _const_spec((V_HEAD_DIM, 1))],
        out_specs=pl.BlockSpec((None, T, V_HEAD_DIM), lambda b, h: (b, 0, h)),
        out_shape=jax.ShapeDtypeStruct((B, T, D_ATTN), bf16),
        scratch_shapes=[pltpu.VMEM((V_HEAD_DIM, 2 * tile), bf16),
                        pltpu.VMEM((V_ROWS, 2 * tile), f32),
                        pltpu.VMEM((1, 2 * tile), f32),
                        pltpu.VMEM((tile, 2 * tile), f32),
                        pltpu.VMEM((tile, 2 * tile), f32),
                        pltpu.VMEM((tile, 2 * tile), f32)],
        compiler_params=pltpu.CompilerParams(
            dimension_semantics=("parallel", "parallel"), vmem_limit_bytes=VMEM_LIMIT),
        name="diff_attn",
    )(qT, k, vT, lq1, lk1, lq2, lk2, g_col)


def _rglru_kernel(xr_ref, gate_ref, cw_ref, cb_ref, wr_ref, wi_ref, brg_ref, big_ref, L_ref,
                  g_ref, o_ref, xbuf_ref, h_ref, *, tc):
    j = pl.program_id(1)

    @pl.when(j == 0)
    def _():
        xbuf_ref[0:SUBLANE, :] = jnp.zeros((SUBLANE, D_REC), f32)
        h_ref[...] = jnp.zeros((1, D_REC), f32)

    x = xr_ref[...]
    xbuf_ref[SUBLANE:SUBLANE + tc, :] = x
    xc = cb_ref[...]
    for t in range(CONV_WIDTH):
        off = SUBLANE - (CONV_WIDTH - 1) + t
        xc = xc + cw_ref[t:t + 1, :] * xbuf_ref[off:off + tc, :]
    xbuf_ref[0:SUBLANE, :] = x[tc - SUBLANE:tc, :]

    xcb = xc.astype(bf16)
    half = D_REC // 2

    def gate_lin(w_ref, b_ref):
        lo = jnp.dot(xcb[:, 0:half], w_ref[0], preferred_element_type=f32)
        hi = jnp.dot(xcb[:, half:D_REC], w_ref[1], preferred_element_type=f32)
        return jnp.concatenate([lo, hi], axis=1) + b_ref[...]

    r = jax.nn.sigmoid(gate_lin(wr_ref, brg_ref))
    i = jax.nn.sigmoid(gate_lin(wi_ref, big_ref))
    log_a = LRU_C * r * jax.nn.log_sigmoid(L_ref[...])
    a = jnp.exp(log_a)
    mult = jnp.sqrt(jnp.maximum(-jnp.tanh(log_a) * (a * a + 1.0), 0.0))
    row = lax.broadcasted_iota(jnp.int32, (tc, 1), 0)
    t_idx = j * tc + row
    mult = jnp.where(t_idx == FRONT_PAD, 1.0, mult)
    u = jnp.where(t_idx >= FRONT_PAD, mult * i * xc, 0.0)

    d = 1
    while d < tc:
        a_s = jnp.where(row >= d, pltpu.roll(a, d, 0), 1.0)
        u_s = jnp.where(row >= d, pltpu.roll(u, d, 0), 0.0)
        u = a * u_s + u
        a = a * a_s
        d *= 2
    hseq = u + a * h_ref[...]
    h_ref[...] = hseq[tc - 1:tc, :]

    y = hseq * jax.nn.gelu(gate_ref[...])
    ms = jnp.mean(y * y, axis=-1, keepdims=True)
    o_ref[...] = (y * lax.rsqrt(ms + EPS) * g_ref[...]).astype(bf16)


def _rglru(xr, gate, cw, cb, wr, wi, brg, big, L, g):
    B, T, _ = xr.shape
    tc = ROW_TILE
    row_blk = pl.BlockSpec((None, tc, D_REC), lambda b, j: (b, j, 0))
    vec = _const_spec((1, D_REC))
    half = D_REC // 2
    return pl.pallas_call(
        functools.partial(_rglru_kernel, tc=tc),
        grid=(B, T // tc),
        in_specs=[row_blk, row_blk, _const_spec((CONV_WIDTH, D_REC)), vec,
                  _const_spec((2, half, half)), _const_spec((2, half, half)),
                  vec, vec, vec, vec],
        out_specs=row_blk,
        out_shape=jax.ShapeDtypeStruct((B, T, D_REC), bf16),
        scratch_shapes=[pltpu.VMEM((SUBLANE + tc, D_REC), f32), pltpu.VMEM((1, D_REC), f32)],
        compiler_params=pltpu.CompilerParams(
            dimension_semantics=("parallel", "arbitrary"), vmem_limit_bytes=VMEM_LIMIT),
        name="rglru",
    )(xr, gate, cw, cb, wr, wi, brg, big, L, g)


def _out_ffn_kernel(h_ref, oa_ref, or_ref, wo_ref, g_ref, wgu_ref, wd_ref, out_ref):
    out_ref[...] = (h_ref[...]
                    + jnp.dot(oa_ref[...], wo_ref[0:D_ATTN, :], preferred_element_type=f32)
                    + jnp.dot(or_ref[...], wo_ref[D_ATTN:D_ATTN + D_REC, :],
                              preferred_element_type=f32))
    h1 = out_ref[...]
    ms = jnp.mean(h1 * h1, axis=-1, keepdims=True)
    hn = (h1 * lax.rsqrt(ms + EPS) * g_ref[...]).astype(bf16)
    ffn = None
    for c0 in range(0, D_FF, FFN_CHUNK):
        gt = jnp.dot(hn, wgu_ref[:, c0:c0 + FFN_CHUNK], preferred_element_type=f32)
        up = jnp.dot(hn, wgu_ref[:, D_FF + c0:D_FF + c0 + FFN_CHUNK],
                     preferred_element_type=f32)
        act = (jax.nn.silu(gt) * up).astype(bf16)
        part = jnp.dot(act, wd_ref[c0:c0 + FFN_CHUNK, :], preferred_element_type=f32)
        ffn = part if ffn is None else ffn + part
    out_ref[...] += ffn


def _out_ffn(h, oa, orec, wo, g, wgu, wd):
    R, D = h.shape
    tm = FFN_ROW_TILE
    row_blk = lambda width: pl.BlockSpec((tm, width), lambda i: (i, 0))
    return pl.pallas_call(
        _out_ffn_kernel,
        grid=(R // tm,),
        in_specs=[row_blk(D), row_blk(D_ATTN), row_blk(D_REC),
                  _const_spec((D_ATTN + D_REC, D)), _const_spec((1, D)),
                  _const_spec((D, 2 * D_FF)), _const_spec((D_FF, D))],
        out_specs=row_blk(D),
        out_shape=jax.ShapeDtypeStruct((R, D), f32),
        compiler_params=pltpu.CompilerParams(
            dimension_semantics=("parallel",), vmem_limit_bytes=VMEM_LIMIT),
        name="out_ffn",
    )(h, oa, orec, wo, g, wgu, wd)


def _rope_tables(T):
    half = ROPE_DIM // 2
    inv = ROPE_THETA ** (-jnp.arange(0, ROPE_DIM, 2, dtype=f32) / ROPE_DIM)
    pos = (jnp.arange(T) - FRONT_PAD).astype(f32)
    ang = pos[:, None] * inv[None, :]
    d = jnp.arange(LANE) % ATTN_HEAD_DIM
    first, second = d < half, (d >= half) & (d < ROPE_DIM)
    cos_l = jnp.cos(ang)[:, d % half]
    sin_l = jnp.sin(ang)[:, d % half]
    cos_t = jnp.where(first | second, cos_l, 1.0)
    sin_a = jnp.where(first, -sin_l, 0.0)
    sin_b = jnp.where(second, sin_l, 0.0)
    return cos_t, sin_a, sin_b


def _block_diag_halves(w):
    nb = REC_BLOCKS // 2
    w4 = w.reshape(2, nb, REC_BLOCK_DIM, REC_BLOCK_DIM)
    eye = jnp.eye(nb, dtype=w.dtype)
    half = D_REC // 2
    return jnp.einsum('hnde,nm->hndme', w4, eye).reshape(2, half, half)


def kernel(x, meta_tokens, norm_mix_g, w_in, q_norm_g, k_norm_g, lambda_q1, lambda_k1,
           lambda_q2, lambda_k2, subln_g, conv_w, conv_b, w_rg, b_rg, w_ig, b_ig, lru_L,
           rec_norm_g, w_out, norm_ffn_g, w_gu, w_down):
    B, S, D = x.shape
    assert D == D_MODEL and S % ATTN_TILE == 0
    T = S + LANE
    assert T % ROW_TILE == 0 and (B * T) % FFN_ROW_TILE == 0

    meta = jnp.broadcast_to(meta_tokens.astype(x.dtype)[None], (B, N_META, D))
    h = jnp.concatenate([jnp.zeros((B, FRONT_PAD, D), x.dtype), meta, x], axis=1)
    cos_t, sin_a, sin_b = _rope_tables(T)
    grp = jnp.arange(D_ATTN) // ATTN_HEAD_DIM
    gmean = jnp.where(grp[:, None] == grp[None, :], 1.0 / ATTN_HEAD_DIM, 0.0).astype(bf16)
    n_grp = D_ATTN // ATTN_HEAD_DIM
    row = lambda v: v.reshape(1, -1)

    for l in range(DEPTH):
        lam_init = 0.8 - 0.6 * math.exp(-0.3 * l)
        qT, k, vT, xr, gate = _in_proj(
            h, row(norm_mix_g[l]), w_in[l].astype(bf16),
            row(jnp.tile(q_norm_g[l], n_grp)), row(jnp.tile(k_norm_g[l], n_grp)),
            cos_t, sin_a, sin_b, gmean)
        o_attn = _attention(qT, k, vT, row(lambda_q1[l]), row(lambda_k1[l]),
                            row(lambda_q2[l]), row(lambda_k2[l]),
                            subln_g[l].reshape(V_HEAD_DIM, 1), lam_init)
        o_rec = _rglru(xr, gate, conv_w[l], row(conv_b[l]),
                       _block_diag_halves(w_rg[l]).astype(bf16),
                       _block_diag_halves(w_ig[l]).astype(bf16),
                       row(b_rg[l]), row(b_ig[l]), row(lru_L[l]), row(rec_norm_g[l]))
        h = _out_ffn(h.reshape(B * T, D), o_attn.reshape(B * T, D_ATTN),
                     o_rec.reshape(B * T, D_REC), w_out[l].astype(bf16),
                     row(norm_ffn_g[l]), w_gu[l].astype(bf16),
                     w_down[l].astype(bf16)).reshape(B, T, D)

    return h[:, MAIN0:, :]
```

```python
import functools
import math

import jax
import jax.numpy as jnp
from jax import lax
from jax.experimental import pallas as pl
from jax.experimental.pallas import tpu as pltpu

D_MODEL = 1024
DEPTH = 2
N_META = 16
D_ATTN = 512
D_REC = 512
ATTN_HEADS = 4
ATTN_HEAD_DIM = 64
V_HEAD_DIM = 2 * ATTN_HEAD_DIM
ROPE_DIM = ATTN_HEAD_DIM // 4
ROPE_THETA = 500000.0
REC_BLOCKS = 8
REC_BLOCK_DIM = D_REC // REC_BLOCKS
CONV_WIDTH = 4
LRU_C = 8.0
D_FF = 2816
EPS = 1e-6
N_IN = 3 * D_ATTN + 2 * D_REC

LANE = 128
SUBLANE = 8
BF16_SUBLANE = 16
FRONT_PAD = LANE - N_META
MAIN0 = LANE
V_ROWS = V_HEAD_DIM + BF16_SUBLANE
NEG = -1e30
MAX_SCORE_BOUND = 50.0
VMEM_LIMIT = 56 * 1024 * 1024

ROW_TILE = 640
FFN_ROW_TILE = 512
FFN_CHUNK = 1408
ATTN_TILE = 512
ATTN_COL_BLOCK = 256

f32 = jnp.float32
bf16 = jnp.bfloat16


def _const_spec(shape):
    zeros = (0,) * len(shape)
    return pl.BlockSpec(shape, lambda *_: zeros, pipeline_mode=pl.Buffered(1))


def _qk_norm_rope(z, gain, gmean_ref, cos, sin_a, sin_b, scale):
    zz = (z * z).astype(bf16)
    half = D_ATTN // 2
    ms = jnp.concatenate(
        [jnp.dot(zz[:, 0:half], gmean_ref[...], preferred_element_type=f32),
         jnp.dot(zz[:, half:D_ATTN], gmean_ref[...], preferred_element_type=f32)], axis=1)
    zn = z * lax.rsqrt(ms + EPS) * gain
    if scale != 1.0:
        zn = zn * scale
    outs = []
    for c in range(D_ATTN // LANE):
        zc = zn[:, c * LANE:(c + 1) * LANE]
        half = ROPE_DIM // 2
        outs.append(zc * cos
                    + pltpu.roll(zc, LANE - half, 1) * sin_a
                    + pltpu.roll(zc, half, 1) * sin_b)
    return jnp.concatenate(outs, axis=1)


def _in_proj_kernel(h_ref, g_ref, w_ref, qg_ref, kg_ref, cos_ref, sa_ref, sb_ref, gmean_ref,
                    qT_ref, k_ref, vT_ref, xr_ref, gate_ref, *, tm):
    j = pl.program_id(1)
    x = h_ref[...]
    row = j * tm + lax.broadcasted_iota(jnp.int32, (tm, 1), 0)
    ms = jnp.mean(x * x, axis=-1, keepdims=True)
    hn = x * lax.rsqrt(ms + EPS) * g_ref[...]
    hn = jnp.where(row >= FRONT_PAD, hn, 0.0).astype(bf16)

    def proj(c0, width):
        return jnp.dot(hn, w_ref[:, c0:c0 + width], preferred_element_type=f32)

    cos, sa, sb = cos_ref[...], sa_ref[...], sb_ref[...]
    q = _qk_norm_rope(proj(0, D_ATTN), qg_ref[...], gmean_ref, cos, sa, sb,
                      ATTN_HEAD_DIM ** -0.5 * math.log2(math.e))
    qT_ref[...] = q.T.astype(bf16)
    k = _qk_norm_rope(proj(D_ATTN, D_ATTN), kg_ref[...], gmean_ref, cos, sa, sb, 1.0)
    k_ref[...] = k.astype(bf16)
    vT = proj(2 * D_ATTN, D_ATTN).T.astype(bf16)
    ones = jnp.ones((BF16_SUBLANE, tm), bf16)
    for hd in range(ATTN_HEADS):
        vT_ref[hd, 0:V_HEAD_DIM, :] = vT[hd * V_HEAD_DIM:(hd + 1) * V_HEAD_DIM, :]
        vT_ref[hd, V_HEAD_DIM:V_ROWS, :] = ones
    xr_ref[...] = proj(3 * D_ATTN, D_REC)
    gate_ref[...] = proj(3 * D_ATTN + D_REC, D_REC)


def _in_proj(h, g, w, qg, kg, cos_t, sa_t, sb_t, gmean):
    B, T, D = h.shape
    tm = ROW_TILE
    nt = T // tm
    row_blk = lambda width: pl.BlockSpec((None, tm, width), lambda b, j: (b, j, 0))
    tab_blk = pl.BlockSpec((tm, LANE), lambda b, j: (j, 0))
    return pl.pallas_call(
        functools.partial(_in_proj_kernel, tm=tm),
        grid=(B, nt),
        in_specs=[row_blk(D), _const_spec((1, D)), _const_spec((D, N_IN)),
                  _const_spec((1, D_ATTN)), _const_spec((1, D_ATTN)),
                  tab_blk, tab_blk, tab_blk, _const_spec((D_ATTN // 2, D_ATTN // 2))],
        out_specs=[pl.BlockSpec((None, D_ATTN, tm), lambda b, j: (b, 0, j)),
                   row_blk(D_ATTN),
                   pl.BlockSpec((None, ATTN_HEADS, V_ROWS, tm), lambda b, j: (b, 0, 0, j)),
                   row_blk(D_REC), row_blk(D_REC)],
        out_shape=[jax.ShapeDtypeStruct((B, D_ATTN, T), bf16),
                   jax.ShapeDtypeStruct((B, T, D_ATTN), bf16),
                   jax.ShapeDtypeStruct((B, ATTN_HEADS, V_ROWS, T), bf16),
                   jax.ShapeDtypeStruct((B, T, D_REC), f32),
                   jax.ShapeDtypeStruct((B, T, D_REC), f32)],
        compiler_params=pltpu.CompilerParams(
            dimension_semantics=("parallel", "parallel"), vmem_limit_bytes=VMEM_LIMIT),
        name="in_proj",
    )(h, g, w, qg, kg, cos_t, sa_t, sb_t, gmean)


def _attn_kernel(flag_ref, bound_ref, qT_ref, k_ref, vT_ref, lq1_ref, lk1_ref, lq2_ref,
                 lk2_ref, g_ref, o_ref,
                 q2_ref, acc_ref, m_ref, sa_ref, sb_ref, bias_ref, pa_ref, pb_ref, *, tile, n_main,
                 lam_init):
    lam = (jnp.exp(jnp.sum(lq1_ref[...] * lk1_ref[...], keepdims=True))
           - jnp.exp(jnp.sum(lq2_ref[...] * lk2_ref[...], keepdims=True)) + lam_init)
    hd = ATTN_HEAD_DIM

    def load_q(qs, w):
        qt = qT_ref[:, pl.ds(qs, w)]
        zero = jnp.zeros((hd, w), bf16)
        q2_ref[0:hd, 0:w] = qt[0:hd]
        q2_ref[hd:2 * hd, 0:w] = zero
        q2_ref[0:hd, w:2 * w] = zero
        q2_ref[hd:2 * hd, w:2 * w] = qt[hd:2 * hd]

    def finish_tile(qs, w):
        acc = acc_ref[:, 0:2 * w]
        denom = acc[V_HEAD_DIM:V_HEAD_DIM + 1, :]
        on = acc[0:V_HEAD_DIM, :] * (1.0 / denom)
        o = on[:, 0:w] - lam * on[:, w:2 * w]
        ms = jnp.mean(o * o, axis=0, keepdims=True)
        y = o * lax.rsqrt(ms + EPS) * g_ref[...] * (1.0 - lam_init)
        o_ref[pl.ds(qs, w), :] = y.T.astype(bf16)

    w0 = LANE
    load_q(0, w0)
    s0 = jnp.dot(k_ref[0:w0, :], q2_ref[:, 0:2 * w0], preferred_element_type=f32)
    kidx = lax.broadcasted_iota(jnp.int32, (w0, 2 * w0), 0)
    col = lax.broadcasted_iota(jnp.int32, (w0, 2 * w0), 1)
    qidx = jnp.where(col >= w0, col - w0, col)
    s0 = jnp.where((kidx >= FRONT_PAD) & (kidx <= qidx), s0, NEG)
    p0 = jnp.exp2(s0 - jnp.max(s0, axis=0, keepdims=True)).astype(bf16)
    acc_ref[:, 0:2 * w0] = jnp.dot(vT_ref[:, 0:w0], p0, preferred_element_type=f32)
    finish_tile(0, w0)

    kk = lax.broadcasted_iota(jnp.int32, (tile, 2 * tile), 0)
    cc = lax.broadcasted_iota(jnp.int32, (tile, 2 * tile), 1)
    bias_ref[...] = jnp.where(kk <= jnp.where(cc >= tile, cc - tile, cc), 0.0, NEG)

    def meta_keys():
        sm = jnp.dot(k_ref[FRONT_PAD:MAIN0, :], q2_ref[...], preferred_element_type=f32)
        m0 = jnp.max(sm, axis=0, keepdims=True)
        pm = jnp.exp2(sm - m0).astype(bf16)
        pm = jnp.concatenate([jnp.zeros((FRONT_PAD, 2 * tile), bf16), pm], axis=0)
        acc_ref[...] = jnp.dot(vT_ref[:, 0:MAIN0], pm, preferred_element_type=f32)
        m_ref[...] = m0

    def scores(ks, s_ref):
        s_ref[...] = jnp.dot(k_ref[pl.ds(ks, tile), :], q2_ref[...],
                             preferred_element_type=f32)

    def softmax_pv(ks, s_ref, diagonal):
        vc = vT_ref[:, pl.ds(ks, tile)]
        for c0 in range(0, 2 * tile, ATTN_COL_BLOCK):
            cs = slice(c0, c0 + ATTN_COL_BLOCK)
            sc = s_ref[:, cs]
            if diagonal:
                sc = sc + bias_ref[:, cs]
            m_prev = m_ref[:, cs]
            m_new = jnp.maximum(m_prev, jnp.max(sc, axis=0, keepdims=True))
            alpha = jnp.exp2(m_prev - m_new)
            pc = jnp.exp2(sc - m_new).astype(bf16)
            acc_ref[:, cs] = alpha * acc_ref[:, cs] + jnp.dot(vc, pc, preferred_element_type=f32)
            m_ref[:, cs] = m_new

    bound = bound_ref[...]

    col_blocks = [slice(c0, c0 + ATTN_COL_BLOCK) for c0 in range(0, 2 * tile, ATTN_COL_BLOCK)]

    def stage_p(kc, cs, p_ref, diagonal):
        sc = jnp.dot(kc, q2_ref[:, cs], preferred_element_type=f32)
        if diagonal:
            sc = sc + bias_ref[:, cs]
        p_ref[:, cs] = jnp.exp2(sc - bound).astype(bf16)

    def stage_pv(vc, cs, p_ref):
        acc_ref[:, cs] += jnp.dot(vc, p_ref[:, cs], preferred_element_type=f32)

    def bounded_tile(qi, carry):
        qs = pl.multiple_of(MAIN0 + qi * tile, LANE)
        load_q(qs, tile)
        sm = jnp.dot(k_ref[FRONT_PAD:MAIN0, :], q2_ref[...], preferred_element_type=f32)
        pm = jnp.exp2(sm - bound).astype(bf16)
        pm = jnp.concatenate([jnp.zeros((FRONT_PAD, 2 * tile), bf16), pm], axis=0)
        acc_ref[...] = jnp.dot(vT_ref[:, 0:MAIN0], pm, preferred_element_type=f32)

        def chunk_start(n):
            return pl.multiple_of(jnp.where(n == 0, qs, MAIN0 + (n - 1) * tile), LANE)

        def first():
            kc = k_ref[pl.ds(qs, tile), :]
            for cs in col_blocks:
                stage_p(kc, cs, pa_ref, True)

        def step(n, cur_ref, next_ref):
            vc = vT_ref[:, pl.ds(chunk_start(n), tile)]
            kc = k_ref[pl.ds(chunk_start(n + 1), tile), :]
            for cs in col_blocks:
                stage_pv(vc, cs, cur_ref)
                stage_p(kc, cs, next_ref, False)

        def last(n, cur_ref):
            vc = vT_ref[:, pl.ds(chunk_start(n), tile)]
            for cs in col_blocks:
                stage_pv(vc, cs, cur_ref)

        first()

        def pair(t, c):
            step(2 * t, pa_ref, pb_ref)
            step(2 * t + 1, pb_ref, pa_ref)
            return c

        lax.fori_loop(0, qi // 2, pair, 0)
        odd = qi % 2 == 1

        @pl.when(odd)
        def _():
            step(qi - 1, pa_ref, pb_ref)
            last(qi, pb_ref)

        @pl.when(jnp.logical_not(odd))
        def _():
            last(qi, pa_ref)

        finish_tile(qs, tile)
        return carry

    def online_tile(qi, carry):
        qs = pl.multiple_of(MAIN0 + qi * tile, LANE)
        load_q(qs, tile)
        meta_keys()
        scores(MAIN0, sa_ref)

        def step(ks, cur_ref, next_ref):
            scores(pl.multiple_of(ks + tile, LANE), next_ref)
            softmax_pv(ks, cur_ref, False)

        def pair(t, c):
            ks = pl.multiple_of(MAIN0 + 2 * t * tile, LANE)
            step(ks, sa_ref, sb_ref)
            step(pl.multiple_of(ks + tile, LANE), sb_ref, sa_ref)
            return c

        lax.fori_loop(0, qi // 2, pair, 0)
        odd = qi % 2 == 1

        @pl.when(odd)
        def _():
            step(pl.multiple_of(qs - tile, LANE), sa_ref, sb_ref)
            softmax_pv(qs, sb_ref, True)

        @pl.when(jnp.logical_not(odd))
        def _():
            softmax_pv(qs, sa_ref, True)

        finish_tile(qs, tile)
        return carry

    use_bounded = flag_ref[0] == 1

    @pl.when(use_bounded)
    def _():
        lax.fori_loop(0, n_main, bounded_tile, 0)

    @pl.when(jnp.logical_not(use_bounded))
    def _():
        lax.fori_loop(0, n_main, online_tile, 0)


def _attention(qT, k, vT, lq1, lk1, lq2, lk2, g_col, bound, lam_init):
    B, _, T = qT.shape
    tile = ATTN_TILE
    n_main = (T - MAIN0) // tile
    vec = _const_spec((1, ATTN_HEAD_DIM))
    flag = (bound <= MAX_SCORE_BOUND).astype(jnp.int32).reshape(1)
    return pl.pallas_call(
        functools.partial(_attn_kernel, tile=tile, n_main=n_main, lam_init=lam_init),
        grid=(B, ATTN_HEADS),
        in_specs=[pl.BlockSpec(memory_space=pltpu.SMEM), _const_spec((1, 1)),
                  pl.BlockSpec((None, V_HEAD_DIM, T), lambda b, h: (b, h, 0)),
                  pl.BlockSpec((None, T, V_HEAD_DIM), lambda b, h: (b, 0, h)),
                  pl.BlockSpec((None, None, V_ROWS, T), lambda b, h: (b, h, 0, 0)),
                  vec, vec, vec, vec, _const_spec((V_HEAD_DIM, 1))],
        out_specs=pl.BlockSpec((None, T, V_HEAD_DIM), lambda b, h: (b, 0, h)),
        out_shape=jax.ShapeDtypeStruct((B, T, D_ATTN), bf16),
        scratch_shapes=[pltpu.VMEM((V_HEAD_DIM, 2 * tile), bf16),
                        pltpu.VMEM((V_ROWS, 2 * tile), f32),
                        pltpu.VMEM((1, 2 * tile), f32),
                        pltpu.VMEM((tile, 2 * tile), f32),
                        pltpu.VMEM((tile, 2 * tile), f32),
                        pltpu.VMEM((tile, 2 * tile), f32),
                        pltpu.VMEM((tile, 2 * tile), bf16),
                        pltpu.VMEM((tile, 2 * tile), bf16)],
        compiler_params=pltpu.CompilerParams(
            dimension_semantics=("parallel", "parallel"), vmem_limit_bytes=VMEM_LIMIT),
        name="diff_attn",
    )(flag, bound.reshape(1, 1), qT, k, vT, lq1, lk1, lq2, lk2, g_col)


def _rglru_kernel(xr_ref, gate_ref, cw_ref, cb_ref, wr_ref, wi_ref, brg_ref, big_ref, L_ref,
                  g_ref, o_ref, xbuf_ref, h_ref, *, tc):
    j = pl.program_id(1)

    @pl.when(j == 0)
    def _():
        xbuf_ref[0:SUBLANE, :] = jnp.zeros((SUBLANE, D_REC), f32)
        h_ref[...] = jnp.zeros((1, D_REC), f32)

    x = xr_ref[...]
    xbuf_ref[SUBLANE:SUBLANE + tc, :] = x
    xc = cb_ref[...]
    for t in range(CONV_WIDTH):
        off = SUBLANE - (CONV_WIDTH - 1) + t
        xc = xc + cw_ref[t:t + 1, :] * xbuf_ref[off:off + tc, :]
    xbuf_ref[0:SUBLANE, :] = x[tc - SUBLANE:tc, :]

    xcb = xc.astype(bf16)
    half = D_REC // 2

    def gate_lin(w_ref, b_ref):
        lo = jnp.dot(xcb[:, 0:half], w_ref[0], preferred_element_type=f32)
        hi = jnp.dot(xcb[:, half:D_REC], w_ref[1], preferred_element_type=f32)
        return jnp.concatenate([lo, hi], axis=1) + b_ref[...]

    r = jax.nn.sigmoid(gate_lin(wr_ref, brg_ref))
    i = jax.nn.sigmoid(gate_lin(wi_ref, big_ref))
    log_a = LRU_C * r * jax.nn.log_sigmoid(L_ref[...])
    a = jnp.exp(log_a)
    mult = jnp.sqrt(jnp.maximum(-jnp.tanh(log_a) * (a * a + 1.0), 0.0))
    row = lax.broadcasted_iota(jnp.int32, (tc, 1), 0)
    t_idx = j * tc + row
    mult = jnp.where(t_idx == FRONT_PAD, 1.0, mult)
    u = jnp.where(t_idx >= FRONT_PAD, mult * i * xc, 0.0)

    d = 1
    while d < tc:
        a_s = jnp.where(row >= d, pltpu.roll(a, d, 0), 1.0)
        u_s = jnp.where(row >= d, pltpu.roll(u, d, 0), 0.0)
        u = a * u_s + u
        a = a * a_s
        d *= 2
    hseq = u + a * h_ref[...]
    h_ref[...] = hseq[tc - 1:tc, :]

    y = hseq * jax.nn.gelu(gate_ref[...])
    ms = jnp.mean(y * y, axis=-1, keepdims=True)
    o_ref[...] = (y * lax.rsqrt(ms + EPS) * g_ref[...]).astype(bf16)


def _rglru(xr, gate, cw, cb, wr, wi, brg, big, L, g):
    B, T, _ = xr.shape
    tc = ROW_TILE
    row_blk = pl.BlockSpec((None, tc, D_REC), lambda b, j: (b, j, 0))
    vec = _const_spec((1, D_REC))
    half = D_REC // 2
    return pl.pallas_call(
        functools.partial(_rglru_kernel, tc=tc),
        grid=(B, T // tc),
        in_specs=[row_blk, row_blk, _const_spec((CONV_WIDTH, D_REC)), vec,
                  _const_spec((2, half, half)), _const_spec((2, half, half)),
                  vec, vec, vec, vec],
        out_specs=row_blk,
        out_shape=jax.ShapeDtypeStruct((B, T, D_REC), bf16),
        scratch_shapes=[pltpu.VMEM((SUBLANE + tc, D_REC), f32), pltpu.VMEM((1, D_REC), f32)],
        compiler_params=pltpu.CompilerParams(
            dimension_semantics=("parallel", "arbitrary"), vmem_limit_bytes=VMEM_LIMIT),
        name="rglru",
    )(xr, gate, cw, cb, wr, wi, brg, big, L, g)


def _out_ffn_kernel(h_ref, oa_ref, or_ref, wo_ref, g_ref, wgu_ref, wd_ref, out_ref, *,
                    leading_unit_dim):
    if leading_unit_dim:
        h_ref, oa_ref, or_ref = h_ref.at[0], oa_ref.at[0], or_ref.at[0]
    out_ref[...] = (h_ref[...]
                    + jnp.dot(oa_ref[...], wo_ref[0:D_ATTN, :], preferred_element_type=f32)
                    + jnp.dot(or_ref[...], wo_ref[D_ATTN:D_ATTN + D_REC, :],
                              preferred_element_type=f32))
    h1 = out_ref[...]
    ms = jnp.mean(h1 * h1, axis=-1, keepdims=True)
    hn = (h1 * lax.rsqrt(ms + EPS) * g_ref[...]).astype(bf16)
    ffn = None
    for c0 in range(0, D_FF, FFN_CHUNK):
        gt = jnp.dot(hn, wgu_ref[:, c0:c0 + FFN_CHUNK], preferred_element_type=f32)
        up = jnp.dot(hn, wgu_ref[:, D_FF + c0:D_FF + c0 + FFN_CHUNK],
                     preferred_element_type=f32)
        act = (jax.nn.silu(gt) * up).astype(bf16)
        part = jnp.dot(act, wd_ref[c0:c0 + FFN_CHUNK, :], preferred_element_type=f32)
        ffn = part if ffn is None else ffn + part
    out_ref[...] += ffn


def _out_ffn(h, oa, orec, wo, g, wgu, wd):
    R, D = h.shape
    tm = FFN_ROW_TILE
    row_blk = lambda width: pl.BlockSpec((tm, width), lambda i: (i, 0))
    return pl.pallas_call(
        functools.partial(_out_ffn_kernel, leading_unit_dim=False),
        grid=(R // tm,),
        in_specs=[row_blk(D), row_blk(D_ATTN), row_blk(D_REC),
                  _const_spec((D_ATTN + D_REC, D)), _const_spec((1, D)),
                  _const_spec((D, 2 * D_FF)), _const_spec((D_FF, D))],
        out_specs=row_blk(D),
        out_shape=jax.ShapeDtypeStruct((R, D), f32),
        compiler_params=pltpu.CompilerParams(
            dimension_semantics=("parallel",), vmem_limit_bytes=VMEM_LIMIT),
        name="out_ffn",
    )(h, oa, orec, wo, g, wgu, wd)


def _out_ffn_last(h, oa, orec, wo, g, wgu, wd):
    B, T, D = h.shape
    S = T - MAIN0
    tm = FFN_ROW_TILE

    def win(width):
        return pl.BlockSpec((pl.Element(1), pl.Element(tm), pl.Element(width)),
                            lambda b, i: (b, pl.multiple_of(MAIN0 + i * tm, LANE), 0))

    return pl.pallas_call(
        functools.partial(_out_ffn_kernel, leading_unit_dim=True),
        grid=(B, S // tm),
        in_specs=[win(D), win(D_ATTN), win(D_REC),
                  _const_spec((D_ATTN + D_REC, D)), _const_spec((1, D)),
                  _const_spec((D, 2 * D_FF)), _const_spec((D_FF, D))],
        out_specs=pl.BlockSpec((None, tm, D), lambda b, i: (b, i, 0)),
        out_shape=jax.ShapeDtypeStruct((B, S, D), f32),
        compiler_params=pltpu.CompilerParams(
            dimension_semantics=("parallel", "parallel"), vmem_limit_bytes=VMEM_LIMIT),
        name="out_ffn_last",
    )(h, oa, orec, wo, g, wgu, wd)


def _rope_tables(T):
    half = ROPE_DIM // 2
    inv = ROPE_THETA ** (-jnp.arange(0, ROPE_DIM, 2, dtype=f32) / ROPE_DIM)
    pos = (jnp.arange(T) - FRONT_PAD).astype(f32)
    ang = pos[:, None] * inv[None, :]
    d = jnp.arange(LANE) % ATTN_HEAD_DIM
    first, second = d < half, (d >= half) & (d < ROPE_DIM)
    cos_l = jnp.cos(ang)[:, d % half]
    sin_l = jnp.sin(ang)[:, d % half]
    cos_t = jnp.where(first | second, cos_l, 1.0)
    sin_a = jnp.where(first, -sin_l, 0.0)
    sin_b = jnp.where(second, sin_l, 0.0)
    return cos_t, sin_a, sin_b


def _block_diag_halves(w):
    nb = REC_BLOCKS // 2
    w4 = w.reshape(2, nb, REC_BLOCK_DIM, REC_BLOCK_DIM)
    eye = jnp.eye(nb, dtype=w.dtype)
    half = D_REC // 2
    return jnp.einsum('hnde,nm->hndme', w4, eye).reshape(2, half, half)


def kernel(x, meta_tokens, norm_mix_g, w_in, q_norm_g, k_norm_g, lambda_q1, lambda_k1,
           lambda_q2, lambda_k2, subln_g, conv_w, conv_b, w_rg, b_rg, w_ig, b_ig, lru_L,
           rec_norm_g, w_out, norm_ffn_g, w_gu, w_down):
    B, S, D = x.shape
    assert D == D_MODEL and S % ATTN_TILE == 0
    T = S + LANE
    assert T % ROW_TILE == 0 and (B * T) % FFN_ROW_TILE == 0

    meta = jnp.broadcast_to(meta_tokens.astype(x.dtype)[None], (B, N_META, D))
    h = jnp.concatenate([jnp.zeros((B, FRONT_PAD, D), x.dtype), meta, x], axis=1)
    cos_t, sin_a, sin_b = _rope_tables(T)
    grp = jnp.arange(D_ATTN // 2) // ATTN_HEAD_DIM
    gmean = jnp.where(grp[:, None] == grp[None, :], 1.0 / ATTN_HEAD_DIM, 0.0).astype(bf16)
    n_grp = D_ATTN // ATTN_HEAD_DIM
    row = lambda v: v.reshape(1, -1)

    for l in range(DEPTH):
        lam_init = 0.8 - 0.6 * math.exp(-0.3 * l)
        qT, k, vT, xr, gate = _in_proj(
            h, row(norm_mix_g[l]), w_in[l].astype(bf16),
            row(jnp.tile(q_norm_g[l], n_grp)), row(jnp.tile(k_norm_g[l], n_grp)),
            cos_t, sin_a, sin_b, gmean)
        bound = (ATTN_HEAD_DIM ** 0.5 * math.log2(math.e)
                 * jnp.max(jnp.abs(q_norm_g[l])) * jnp.max(jnp.abs(k_norm_g[l])))
        o_attn = _attention(qT, k, vT, row(lambda_q1[l]), row(lambda_k1[l]),
                            row(lambda_q2[l]), row(lambda_k2[l]),
                            subln_g[l].reshape(V_HEAD_DIM, 1), bound, lam_init)
        o_rec = _rglru(xr, gate, conv_w[l], row(conv_b[l]),
                       _block_diag_halves(w_rg[l]).astype(bf16),
                       _block_diag_halves(w_ig[l]).astype(bf16),
                       row(b_rg[l]), row(b_ig[l]), row(lru_L[l]), row(rec_norm_g[l]))
        ffn_w = (w_out[l].astype(bf16), row(norm_ffn_g[l]), w_gu[l].astype(bf16),
                 w_down[l].astype(bf16))
        if l == DEPTH - 1:
            return _out_ffn_last(h, o_attn, o_rec, *ffn_w)
        h = _out_ffn(h.reshape(B * T, D), o_attn.reshape(B * T, D_ATTN),
                     o_rec.reshape(B * T, D_REC), *ffn_w).reshape(B, T, D)
```

```python
import functools
import math

import jax
import jax.numpy as jnp
from jax import lax
from jax.experimental import pallas as pl
from jax.experimental.pallas import tpu as pltpu

D_MODEL = 1024
DEPTH = 2
N_META = 16
D_ATTN = 512
D_REC = 512
ATTN_HEADS = 4
ATTN_HEAD_DIM = 64
V_HEAD_DIM = 2 * ATTN_HEAD_DIM
ROPE_DIM = ATTN_HEAD_DIM // 4
ROPE_THETA = 500000.0
REC_BLOCKS = 8
REC_BLOCK_DIM = D_REC // REC_BLOCKS
CONV_WIDTH = 4
LRU_C = 8.0
D_FF = 2816
EPS = 1e-6
N_IN = 3 * D_ATTN + 2 * D_REC

LANE = 128
SUBLANE = 8
BF16_SUBLANE = 16
FRONT_PAD = LANE - N_META
MAIN0 = LANE
V_ROWS = V_HEAD_DIM + BF16_SUBLANE
NEG = -1e30
MAX_SCORE_BOUND = 50.0
VMEM_LIMIT = 56 * 1024 * 1024

ROW_TILE = 640
SCAN_UNROLL = 8
FFN_ROW_TILE = 512
FFN_CHUNK = 256
ATTN_TILE = 512
ATTN_COL_BLOCK = 256

f32 = jnp.float32
bf16 = jnp.bfloat16


def _const_spec(shape):
    zeros = (0,) * len(shape)
    return pl.BlockSpec(shape, lambda *_: zeros, pipeline_mode=pl.Buffered(1))


def _qk_norm_rope(z, gain, gmean_ref, cos, sin_a, sin_b, scale):
    zz = (z * z).astype(bf16)
    half = D_ATTN // 2
    ms = jnp.concatenate(
        [jnp.dot(zz[:, 0:half], gmean_ref[...], preferred_element_type=f32),
         jnp.dot(zz[:, half:D_ATTN], gmean_ref[...], preferred_element_type=f32)], axis=1)
    zn = z * lax.rsqrt(ms + EPS) * gain
    if scale != 1.0:
        zn = zn * scale
    outs = []
    for c in range(D_ATTN // LANE):
        zc = zn[:, c * LANE:(c + 1) * LANE]
        half = ROPE_DIM // 2
        outs.append(zc * cos
                    + pltpu.roll(zc, LANE - half, 1) * sin_a
                    + pltpu.roll(zc, half, 1) * sin_b)
    return jnp.concatenate(outs, axis=1)


def _in_proj_kernel(h_ref, g_ref, w_ref, qg_ref, kg_ref, cos_ref, sa_ref, sb_ref, gmean_ref,
                    qT_ref, k_ref, vT_ref, xr_ref, gate_ref, *, tm):
    j = pl.program_id(1)
    x = h_ref[...]
    row = j * tm + lax.broadcasted_iota(jnp.int32, (tm, 1), 0)
    ms = jnp.mean(x * x, axis=-1, keepdims=True)
    hn = x * lax.rsqrt(ms + EPS) * g_ref[...]
    hn = jnp.where(row >= FRONT_PAD, hn, 0.0).astype(bf16)

    def proj(c0, width):
        return jnp.dot(hn, w_ref[:, c0:c0 + width], preferred_element_type=f32)

    cos, sa, sb = cos_ref[...], sa_ref[...], sb_ref[...]
    q = _qk_norm_rope(proj(0, D_ATTN), qg_ref[...], gmean_ref, cos, sa, sb,
                      ATTN_HEAD_DIM ** -0.5 * math.log2(math.e))
    qT_ref[...] = q.T.astype(bf16)
    k = _qk_norm_rope(proj(D_ATTN, D_ATTN), kg_ref[...], gmean_ref, cos, sa, sb, 1.0)
    k_ref[...] = k.astype(bf16)
    vT = proj(2 * D_ATTN, D_ATTN).T.astype(bf16)
    ones = jnp.ones((BF16_SUBLANE, tm), bf16)
    for hd in range(ATTN_HEADS):
        vT_ref[hd, 0:V_HEAD_DIM, :] = vT[hd * V_HEAD_DIM:(hd + 1) * V_HEAD_DIM, :]
        vT_ref[hd, V_HEAD_DIM:V_ROWS, :] = ones
    xr_ref[...] = proj(3 * D_ATTN, D_REC)
    gate_ref[...] = proj(3 * D_ATTN + D_REC, D_REC)


def _in_proj(h, g, w, qg, kg, cos_t, sa_t, sb_t, gmean):
    B, T, D = h.shape
    tm = ROW_TILE
    nt = T // tm
    row_blk = lambda width: pl.BlockSpec((None, tm, width), lambda b, j: (b, j, 0))
    tab_blk = pl.BlockSpec((tm, LANE), lambda b, j: (j, 0))
    return pl.pallas_call(
        functools.partial(_in_proj_kernel, tm=tm),
        grid=(B, nt),
        in_specs=[row_blk(D), _const_spec((1, D)), _const_spec((D, N_IN)),
                  _const_spec((1, D_ATTN)), _const_spec((1, D_ATTN)),
                  tab_blk, tab_blk, tab_blk, _const_spec((D_ATTN // 2, D_ATTN // 2))],
        out_specs=[pl.BlockSpec((None, D_ATTN, tm), lambda b, j: (b, 0, j)),
                   row_blk(D_ATTN),
                   pl.BlockSpec((None, ATTN_HEADS, V_ROWS, tm), lambda b, j: (b, 0, 0, j)),
                   row_blk(D_REC), row_blk(D_REC)],
        out_shape=[jax.ShapeDtypeStruct((B, D_ATTN, T), bf16),
                   jax.ShapeDtypeStruct((B, T, D_ATTN), bf16),
                   jax.ShapeDtypeStruct((B, ATTN_HEADS, V_ROWS, T), bf16),
                   jax.ShapeDtypeStruct((B, T, D_REC), f32),
                   jax.ShapeDtypeStruct((B, T, D_REC), f32)],
        compiler_params=pltpu.CompilerParams(
            dimension_semantics=("parallel", "parallel"), vmem_limit_bytes=VMEM_LIMIT),
        name="in_proj",
    )(h, g, w, qg, kg, cos_t, sa_t, sb_t, gmean)


def _attn_kernel(flag_ref, bound_ref, qT_ref, k_ref, vT_ref, lq1_ref, lk1_ref, lq2_ref,
                 lk2_ref, g_ref, o_ref,
                 q2_ref, acc_ref, m_ref, sa_ref, sb_ref, bias_ref, pa_ref, pb_ref, *, tile, big,
                 n_main, lam_init):
    lam = (jnp.exp(jnp.sum(lq1_ref[...] * lk1_ref[...], keepdims=True))
           - jnp.exp(jnp.sum(lq2_ref[...] * lk2_ref[...], keepdims=True)) + lam_init)
    hd = ATTN_HEAD_DIM

    def load_q(qs, w):
        qt = qT_ref[:, pl.ds(qs, w)]
        zero = jnp.zeros((hd, w), bf16)
        q2_ref[0:hd, 0:w] = qt[0:hd]
        q2_ref[hd:2 * hd, 0:w] = zero
        q2_ref[0:hd, w:2 * w] = zero
        q2_ref[hd:2 * hd, w:2 * w] = qt[hd:2 * hd]

    def finish_tile(qs, w):
        acc = acc_ref[:, 0:2 * w]
        denom = acc[V_HEAD_DIM:V_HEAD_DIM + 1, :]
        on = acc[0:V_HEAD_DIM, :] * (1.0 / denom)
        o = on[:, 0:w] - lam * on[:, w:2 * w]
        ms = jnp.mean(o * o, axis=0, keepdims=True)
        y = o * lax.rsqrt(ms + EPS) * g_ref[...] * (1.0 - lam_init)
        o_ref[pl.ds(qs, w), :] = y.T.astype(bf16)

    w0 = LANE
    load_q(0, w0)
    s0 = jnp.dot(k_ref[0:w0, :], q2_ref[:, 0:2 * w0], preferred_element_type=f32)
    kidx = lax.broadcasted_iota(jnp.int32, (w0, 2 * w0), 0)
    col = lax.broadcasted_iota(jnp.int32, (w0, 2 * w0), 1)
    qidx = jnp.where(col >= w0, col - w0, col)
    s0 = jnp.where((kidx >= FRONT_PAD) & (kidx <= qidx), s0, NEG)
    p0 = jnp.exp2(s0 - jnp.max(s0, axis=0, keepdims=True)).astype(bf16)
    acc_ref[:, 0:2 * w0] = jnp.dot(vT_ref[:, 0:w0], p0, preferred_element_type=f32)
    finish_tile(0, w0)

    kk = lax.broadcasted_iota(jnp.int32, (tile, tile), 0)
    cc = lax.broadcasted_iota(jnp.int32, (tile, tile), 1)
    bias_ref[...] = jnp.where(kk <= cc, 0.0, NEG)

    def tri_bias(cs):
        c0 = cs.start % tile
        return bias_ref[:, c0:c0 + (cs.stop - cs.start)]

    def meta_keys():
        sm = jnp.dot(k_ref[FRONT_PAD:MAIN0, :], q2_ref[:, 0:2 * tile],
                     preferred_element_type=f32)
        m0 = jnp.max(sm, axis=0, keepdims=True)
        pm = jnp.exp2(sm - m0).astype(bf16)
        pm = jnp.concatenate([jnp.zeros((FRONT_PAD, 2 * tile), bf16), pm], axis=0)
        acc_ref[:, 0:2 * tile] = jnp.dot(vT_ref[:, 0:MAIN0], pm, preferred_element_type=f32)
        m_ref[...] = m0

    def scores(ks, s_ref):
        s_ref[...] = jnp.dot(k_ref[pl.ds(ks, tile), :], q2_ref[:, 0:2 * tile],
                             preferred_element_type=f32)

    def softmax_pv(ks, s_ref, diagonal):
        vc = vT_ref[:, pl.ds(ks, tile)]
        for c0 in range(0, 2 * tile, ATTN_COL_BLOCK):
            cs = slice(c0, c0 + ATTN_COL_BLOCK)
            sc = s_ref[:, cs]
            if diagonal:
                sc = sc + tri_bias(cs)
            m_prev = m_ref[:, cs]
            m_new = jnp.maximum(m_prev, jnp.max(sc, axis=0, keepdims=True))
            alpha = jnp.exp2(m_prev - m_new)
            pc = jnp.exp2(sc - m_new).astype(bf16)
            acc_ref[:, cs] = alpha * acc_ref[:, cs] + jnp.dot(vc, pc, preferred_element_type=f32)
            m_ref[:, cs] = m_new

    bound = bound_ref[...]

    def sub_blocks(r):
        return [slice(c * big + r * tile + c0, c * big + r * tile + c0 + ATTN_COL_BLOCK)
                for c in range(2) for c0 in range(0, tile, ATTN_COL_BLOCK)]

    all_blocks = sub_blocks(0) + sub_blocks(1)

    def stage_p(kc, cs, p_ref, diagonal):
        sc = jnp.dot(kc, q2_ref[:, cs], preferred_element_type=f32)
        if diagonal:
            sc = sc + tri_bias(cs)
        p_ref[:, cs] = jnp.exp2(sc - bound).astype(bf16)

    def stage_pv(vc, cs, p_ref):
        acc_ref[:, cs] += jnp.dot(vc, p_ref[:, cs], preferred_element_type=f32)

    def bounded_tile(ti, carry):
        qs = pl.multiple_of(MAIN0 + ti * big, LANE)
        load_q(qs, big)
        sm = jnp.dot(k_ref[FRONT_PAD:MAIN0, :], q2_ref[...], preferred_element_type=f32)
        pm = jnp.exp2(sm - bound).astype(bf16)
        pm = jnp.concatenate([jnp.zeros((FRONT_PAD, 2 * big), bf16), pm], axis=0)
        acc_ref[...] = jnp.dot(vT_ref[:, 0:MAIN0], pm, preferred_element_type=f32)

        def full_start(j):
            return pl.multiple_of(MAIN0 + j * tile, LANE)

        kc = k_ref[pl.ds(qs, tile), :]
        for cs in sub_blocks(0):
            stage_p(kc, cs, pa_ref, True)
        for cs in sub_blocks(1):
            stage_p(kc, cs, pa_ref, False)

        def step(v_start, k_start, cur_ref, next_ref):
            vc = vT_ref[:, pl.ds(v_start, tile)]
            kc = k_ref[pl.ds(k_start, tile), :]
            for cs in all_blocks:
                stage_pv(vc, cs, cur_ref)
                stage_p(kc, cs, next_ref, False)

        def pair(t, c):
            held = pl.multiple_of(jnp.where(t == 0, qs, MAIN0 + (2 * t - 1) * tile), LANE)
            step(held, full_start(2 * t), pa_ref, pb_ref)
            step(full_start(2 * t), full_start(2 * t + 1), pb_ref, pa_ref)
            return c

        lax.fori_loop(0, ti, pair, 0)

        held = pl.multiple_of(jnp.where(ti == 0, qs, MAIN0 + (2 * ti - 1) * tile), LANE)
        vc = vT_ref[:, pl.ds(held, tile)]
        kd = k_ref[pl.ds(qs + tile, tile), :]
        for cs in sub_blocks(0):
            stage_pv(vc, cs, pa_ref)
        for cs in sub_blocks(1):
            stage_pv(vc, cs, pa_ref)
            stage_p(kd, cs, pb_ref, True)
        vd = vT_ref[:, pl.ds(qs + tile, tile)]
        for cs in sub_blocks(1):
            stage_pv(vd, cs, pb_ref)
        finish_tile(qs, big)
        return carry

    def online_tile(qi, carry):
        qs = pl.multiple_of(MAIN0 + qi * tile, LANE)
        load_q(qs, tile)
        meta_keys()
        scores(MAIN0, sa_ref)

        def step(ks, cur_ref, next_ref):
            scores(pl.multiple_of(ks + tile, LANE), next_ref)
            softmax_pv(ks, cur_ref, False)

        def pair(t, c):
            ks = pl.multiple_of(MAIN0 + 2 * t * tile, LANE)
            step(ks, sa_ref, sb_ref)
            step(pl.multiple_of(ks + tile, LANE), sb_ref, sa_ref)
            return c

        lax.fori_loop(0, qi // 2, pair, 0)
        odd = qi % 2 == 1

        @pl.when(odd)
        def _():
            step(pl.multiple_of(qs - tile, LANE), sa_ref, sb_ref)
            softmax_pv(qs, sb_ref, True)

        @pl.when(jnp.logical_not(odd))
        def _():
            softmax_pv(qs, sa_ref, True)

        finish_tile(qs, tile)
        return carry

    use_bounded = flag_ref[0] == 1

    @pl.when(use_bounded)
    def _():
        lax.fori_loop(0, n_main * tile // big, bounded_tile, 0)

    @pl.when(jnp.logical_not(use_bounded))
    def _():
        lax.fori_loop(0, n_main, online_tile, 0)


def _attention(qT, k, vT, lq1, lk1, lq2, lk2, g_col, bound, lam_init):
    B, _, T = qT.shape
    tile, big = ATTN_TILE, 2 * ATTN_TILE
    assert (T - MAIN0) % big == 0
    n_main = (T - MAIN0) // tile
    vec = _const_spec((1, ATTN_HEAD_DIM))
    flag = (bound <= MAX_SCORE_BOUND).astype(jnp.int32).reshape(1)
    return pl.pallas_call(
        functools.partial(_attn_kernel, tile=tile, big=big, n_main=n_main, lam_init=lam_init),
        grid=(B, ATTN_HEADS),
        in_specs=[pl.BlockSpec(memory_space=pltpu.SMEM), _const_spec((1, 1)),
                  pl.BlockSpec((None, V_HEAD_DIM, T), lambda b, h: (b, h, 0)),
                  pl.BlockSpec((None, T, V_HEAD_DIM), lambda b, h: (b, 0, h)),
                  pl.BlockSpec((None, None, V_ROWS, T), lambda b, h: (b, h, 0, 0)),
                  vec, vec, vec, vec, _const_spec((V_HEAD_DIM, 1))],
        out_specs=pl.BlockSpec((None, T, V_HEAD_DIM), lambda b, h: (b, 0, h)),
        out_shape=jax.ShapeDtypeStruct((B, T, D_ATTN), bf16),
        scratch_shapes=[pltpu.VMEM((V_HEAD_DIM, 2 * big), bf16),
                        pltpu.VMEM((V_ROWS, 2 * big), f32),
                        pltpu.VMEM((1, 2 * tile), f32),
                        pltpu.VMEM((tile, 2 * tile), f32),
                        pltpu.VMEM((tile, 2 * tile), f32),
                        pltpu.VMEM((tile, tile), f32),
                        pltpu.VMEM((tile, 2 * big), bf16),
                        pltpu.VMEM((tile, 2 * big), bf16)],
        compiler_params=pltpu.CompilerParams(
            dimension_semantics=("parallel", "parallel"), vmem_limit_bytes=VMEM_LIMIT),
        name="diff_attn",
    )(flag, bound.reshape(1, 1), qT, k, vT, lq1, lk1, lq2, lk2, g_col)


def _rglru_kernel(xr_ref, gate_ref, cw_ref, cb_ref, wr_ref, wi_ref, brg_ref, big_ref, L_ref,
                  g_ref, o_ref, xtail_ref, h_ref, ga_ref, gu_ref, hs_ref, *, tc):
    j = pl.program_id(1)

    @pl.when(j == 0)
    def _():
        xtail_ref[...] = jnp.zeros((SUBLANE, D_REC), f32)
        h_ref[...] = jnp.zeros((1, D_REC), f32)

    x = xr_ref[...]
    ng = tc // SUBLANE
    x3 = x.reshape(ng, SUBLANE, D_REC)
    prev3 = xtail_ref[...].reshape(1, SUBLANE, D_REC)
    sub3 = lax.broadcasted_iota(jnp.int32, (1, SUBLANE, 1), 1)
    xc3 = cw_ref[CONV_WIDTH - 1:CONV_WIDTH, :] * x3
    for k in range(1, CONV_WIDTH):
        rot = pltpu.roll(jnp.concatenate([prev3, x3], axis=0), k, 1)
        shifted = jnp.where(sub3 < k, rot[0:ng], rot[1:ng + 1])
        xc3 = xc3 + cw_ref[CONV_WIDTH - 1 - k:CONV_WIDTH - k, :] * shifted
    xc = xc3.reshape(tc, D_REC) + cb_ref[...]
    xtail_ref[...] = x[tc - SUBLANE:tc, :]

    xcb = xc.astype(bf16)
    half = D_REC // 2

    def gate_lin(w_ref, b_ref):
        lo = jnp.dot(xcb[:, 0:half], w_ref[0], preferred_element_type=f32)
        hi = jnp.dot(xcb[:, half:D_REC], w_ref[1], preferred_element_type=f32)
        return jnp.concatenate([lo, hi], axis=1) + b_ref[...]

    r = jax.nn.sigmoid(gate_lin(wr_ref, brg_ref))
    i = jax.nn.sigmoid(gate_lin(wi_ref, big_ref))
    log_a = LRU_C * r * jax.nn.log_sigmoid(L_ref[...])
    a = jnp.exp(log_a)
    mult = jnp.sqrt(jnp.maximum(-jnp.tanh(log_a) * (a * a + 1.0), 0.0))
    row = lax.broadcasted_iota(jnp.int32, (tc, 1), 0)
    t_idx = j * tc + row
    mult = jnp.where(t_idx == FRONT_PAD, 1.0, mult)
    u = jnp.where(t_idx >= FRONT_PAD, mult * i * xc, 0.0)

    a3 = a.reshape(ng, SUBLANE, D_REC)
    u3 = u.reshape(ng, SUBLANE, D_REC)
    d = 1
    while d < SUBLANE:
        keep = sub3 >= d
        a_s = jnp.where(keep, pltpu.roll(a3, d, 1), 1.0)
        u_s = jnp.where(keep, pltpu.roll(u3, d, 1), 0.0)
        u3 = a3 * u_s + u3
        a3 = a3 * a_s
        d *= 2
    ga_ref[...] = a3.reshape(tc, D_REC)
    gu_ref[...] = u3.reshape(tc, D_REC)

    def carry_group(gi, h):
        r0 = pl.multiple_of(gi * SUBLANE, SUBLANE)
        hg = gu_ref[pl.ds(r0, SUBLANE), :] + ga_ref[pl.ds(r0, SUBLANE), :] * h
        hs_ref[pl.ds(r0, SUBLANE), :] = hg
        return hg[SUBLANE - 1:SUBLANE, :]

    h_ref[...] = lax.fori_loop(0, tc // SUBLANE, carry_group, h_ref[...], unroll=SCAN_UNROLL)
    hseq = hs_ref[...]

    y = hseq * jax.nn.gelu(gate_ref[...])
    ms = jnp.mean(y * y, axis=-1, keepdims=True)
    o_ref[...] = (y * lax.rsqrt(ms + EPS) * g_ref[...]).astype(bf16)


def _rglru(xr, gate, cw, cb, wr, wi, brg, big, L, g):
    B, T, _ = xr.shape
    tc = ROW_TILE
    row_blk = pl.BlockSpec((None, tc, D_REC), lambda b, j: (b, j, 0))
    vec = _const_spec((1, D_REC))
    half = D_REC // 2
    return pl.pallas_call(
        functools.partial(_rglru_kernel, tc=tc),
        grid=(B, T // tc),
        in_specs=[row_blk, row_blk, _const_spec((CONV_WIDTH, D_REC)), vec,
                  _const_spec((2, half, half)), _const_spec((2, half, half)),
                  vec, vec, vec, vec],
        out_specs=row_blk,
        out_shape=jax.ShapeDtypeStruct((B, T, D_REC), bf16),
        scratch_shapes=[pltpu.VMEM((SUBLANE, D_REC), f32),
                        pltpu.VMEM((1, D_REC), f32),
                        pltpu.VMEM((tc, D_REC), f32),
                        pltpu.VMEM((tc, D_REC), f32),
                        pltpu.VMEM((tc, D_REC), f32)],
        compiler_params=pltpu.CompilerParams(
            dimension_semantics=("parallel", "arbitrary"), vmem_limit_bytes=VMEM_LIMIT),
        name="rglru",
    )(xr, gate, cw, cb, wr, wi, brg, big, L, g)


def _out_ffn_kernel(h_ref, oa_ref, or_ref, wo_ref, g_ref, wgu_ref, wd_ref, out_ref, *,
                    leading_unit_dim):
    if leading_unit_dim:
        h_ref, oa_ref, or_ref = h_ref.at[0], oa_ref.at[0], or_ref.at[0]
    out_ref[...] = (h_ref[...]
                    + jnp.dot(oa_ref[...], wo_ref[0:D_ATTN, :], preferred_element_type=f32)
                    + jnp.dot(or_ref[...], wo_ref[D_ATTN:D_ATTN + D_REC, :],
                              preferred_element_type=f32))
    h1 = out_ref[...]
    ms = jnp.mean(h1 * h1, axis=-1, keepdims=True)
    hn = (h1 * lax.rsqrt(ms + EPS) * g_ref[...]).astype(bf16)
    ffn = None
    for c0 in range(0, D_FF, FFN_CHUNK):
        gt = jnp.dot(hn, wgu_ref[:, c0:c0 + FFN_CHUNK], preferred_element_type=f32)
        up = jnp.dot(hn, wgu_ref[:, D_FF + c0:D_FF + c0 + FFN_CHUNK],
                     preferred_element_type=f32)
        act = (jax.nn.silu(gt) * up).astype(bf16)
        part = jnp.dot(act, wd_ref[c0:c0 + FFN_CHUNK, :], preferred_element_type=f32)
        ffn = part if ffn is None else ffn + part
    out_ref[...] += ffn


def _out_ffn(h, oa, orec, wo, g, wgu, wd):
    R, D = h.shape
    tm = FFN_ROW_TILE
    row_blk = lambda width: pl.BlockSpec((tm, width), lambda i: (i, 0))
    return pl.pallas_call(
        functools.partial(_out_ffn_kernel, leading_unit_dim=False),
        grid=(R // tm,),
        in_specs=[row_blk(D), row_blk(D_ATTN), row_blk(D_REC),
                  _const_spec((D_ATTN + D_REC, D)), _const_spec((1, D)),
                  _const_spec((D, 2 * D_FF)), _const_spec((D_FF, D))],
        out_specs=row_blk(D),
        out_shape=jax.ShapeDtypeStruct((R, D), f32),
        compiler_params=pltpu.CompilerParams(
            dimension_semantics=("parallel",), vmem_limit_bytes=VMEM_LIMIT),
        name="out_ffn",
    )(h, oa, orec, wo, g, wgu, wd)


def _out_ffn_last(h, oa, orec, wo, g, wgu, wd):
    B, T, D = h.shape
    S = T - MAIN0
    tm = FFN_ROW_TILE

    def win(width):
        return pl.BlockSpec((pl.Element(1), pl.Element(tm), pl.Element(width)),
                            lambda b, i: (b, pl.multiple_of(MAIN0 + i * tm, LANE), 0))

    return pl.pallas_call(
        functools.partial(_out_ffn_kernel, leading_unit_dim=True),
        grid=(B, S // tm),
        in_specs=[win(D), win(D_ATTN), win(D_REC),
                  _const_spec((D_ATTN + D_REC, D)), _const_spec((1, D)),
                  _const_spec((D, 2 * D_FF)), _const_spec((D_FF, D))],
        out_specs=pl.BlockSpec((None, tm, D), lambda b, i: (b, i, 0)),
        out_shape=jax.ShapeDtypeStruct((B, S, D), f32),
        compiler_params=pltpu.CompilerParams(
            dimension_semantics=("parallel", "parallel"), vmem_limit_bytes=VMEM_LIMIT),
        name="out_ffn_last",
    )(h, oa, orec, wo, g, wgu, wd)


def _rope_tables(T):
    half = ROPE_DIM // 2
    inv = ROPE_THETA ** (-jnp.arange(0, ROPE_DIM, 2, dtype=f32) / ROPE_DIM)
    pos = (jnp.arange(T) - FRONT_PAD).astype(f32)
    ang = pos[:, None] * inv[None, :]
    d = jnp.arange(LANE) % ATTN_HEAD_DIM
    first, second = d < half, (d >= half) & (d < ROPE_DIM)
    cos_l = jnp.cos(ang)[:, d % half]
    sin_l = jnp.sin(ang)[:, d % half]
    cos_t = jnp.where(first | second, cos_l, 1.0)
    sin_a = jnp.where(first, -sin_l, 0.0)
    sin_b = jnp.where(second, sin_l, 0.0)
    return cos_t, sin_a, sin_b


def _block_diag_halves(w):
    nb = REC_BLOCKS // 2
    w4 = w.reshape(2, nb, REC_BLOCK_DIM, REC_BLOCK_DIM)
    eye = jnp.eye(nb, dtype=w.dtype)
    half = D_REC // 2
    return jnp.einsum('hnde,nm->hndme', w4, eye).reshape(2, half, half)


def kernel(x, meta_tokens, norm_mix_g, w_in, q_norm_g, k_norm_g, lambda_q1, lambda_k1,
           lambda_q2, lambda_k2, subln_g, conv_w, conv_b, w_rg, b_rg, w_ig, b_ig, lru_L,
           rec_norm_g, w_out, norm_ffn_g, w_gu, w_down):
    B, S, D = x.shape
    assert D == D_MODEL and S % ATTN_TILE == 0
    T = S + LANE
    assert T % ROW_TILE == 0 and (B * T) % FFN_ROW_TILE == 0

    meta = jnp.broadcast_to(meta_tokens.astype(x.dtype)[None], (B, N_META, D))
    h = jnp.concatenate([jnp.zeros((B, FRONT_PAD, D), x.dtype), meta, x], axis=1)
    cos_t, sin_a, sin_b = _rope_tables(T)
    grp = jnp.arange(D_ATTN // 2) // ATTN_HEAD_DIM
    gmean = jnp.where(grp[:, None] == grp[None, :], 1.0 / ATTN_HEAD_DIM, 0.0).astype(bf16)
    n_grp = D_ATTN // ATTN_HEAD_DIM
    row = lambda v: v.reshape(1, -1)

    for l in range(DEPTH):
        lam_init = 0.8 - 0.6 * math.exp(-0.3 * l)
        qT, k, vT, xr, gate = _in_proj(
            h, row(norm_mix_g[l]), w_in[l].astype(bf16),
            row(jnp.tile(q_norm_g[l], n_grp)), row(jnp.tile(k_norm_g[l], n_grp)),
            cos_t, sin_a, sin_b, gmean)
        bound = (ATTN_HEAD_DIM ** 0.5 * math.log2(math.e)
                 * jnp.max(jnp.abs(q_norm_g[l])) * jnp.max(jnp.abs(k_norm_g[l])))
        o_attn = _attention(qT, k, vT, row(lambda_q1[l]), row(lambda_k1[l]),
                            row(lambda_q2[l]), row(lambda_k2[l]),
                            subln_g[l].reshape(V_HEAD_DIM, 1), bound, lam_init)
        o_rec = _rglru(xr, gate, conv_w[l], row(conv_b[l]),
                       _block_diag_halves(w_rg[l]).astype(bf16),
                       _block_diag_halves(w_ig[l]).astype(bf16),
                       row(b_rg[l]), row(b_ig[l]), row(lru_L[l]), row(rec_norm_g[l]))
        ffn_w = (w_out[l].astype(bf16), row(norm_ffn_g[l]), w_gu[l].astype(bf16),
                 w_down[l].astype(bf16))
        if l == DEPTH - 1:
            return _out_ffn_last(h, o_attn, o_rec, *ffn_w)
        h = _out_ffn(h.reshape(B * T, D), o_attn.reshape(B * T, D_ATTN),
                     o_rec.reshape(B * T, D_REC), *ffn_w).reshape(B, T, D)
```

```python
import functools
import math

import jax
import jax.numpy as jnp
from jax import lax
from jax.experimental import pallas as pl
from jax.experimental.pallas import tpu as pltpu

D_MODEL = 1024
DEPTH = 2
N_META = 16
D_ATTN = 512
D_REC = 512
ATTN_HEADS = 4
ATTN_HEAD_DIM = 64
V_HEAD_DIM = 2 * ATTN_HEAD_DIM
ROPE_DIM = ATTN_HEAD_DIM // 4
ROPE_THETA = 500000.0
REC_BLOCKS = 8
REC_BLOCK_DIM = D_REC // REC_BLOCKS
CONV_WIDTH = 4
LRU_C = 8.0
D_FF = 2816
EPS = 1e-6
N_IN = 3 * D_ATTN + 2 * D_REC

LANE = 128
SUBLANE = 8
FRONT_PAD = LANE - N_META
MAIN0 = LANE
NEG = -1e30
MAX_SCORE_BOUND = 50.0
VMEM_LIMIT = 56 * 1024 * 1024

ROW_TILE = 640
SCAN_UNROLL = 8
FFN_ROW_TILE = 512
FFN_CHUNK = 256
ATTN_TILE = 512
ATTN_COL_BLOCK = 256

f32 = jnp.float32
bf16 = jnp.bfloat16


def _const_spec(shape, layer=None):
    zeros = (0,) * len(shape)
    if layer is None:
        return pl.BlockSpec(shape, lambda *_: zeros, pipeline_mode=pl.Buffered(1))
    return pl.BlockSpec((None,) + tuple(shape), lambda *_: (layer,) + zeros,
                        pipeline_mode=pl.Buffered(1))


def _qk_norm_rope(z, gain, gmean_ref, cos, sin_a, sin_b, scale):
    zz = (z * z).astype(bf16)
    half = D_ATTN // 2
    ms = jnp.concatenate(
        [jnp.dot(zz[:, 0:half], gmean_ref[...], preferred_element_type=f32),
         jnp.dot(zz[:, half:D_ATTN], gmean_ref[...], preferred_element_type=f32)], axis=1)
    zn = z * lax.rsqrt(ms + EPS) * gain
    if scale != 1.0:
        zn = zn * scale
    outs = []
    for c in range(D_ATTN // LANE):
        zc = zn[:, c * LANE:(c + 1) * LANE]
        half = ROPE_DIM // 2
        outs.append(zc * cos
                    + pltpu.roll(zc, LANE - half, 1) * sin_a
                    + pltpu.roll(zc, half, 1) * sin_b)
    return jnp.concatenate(outs, axis=1)


def _in_proj_first_kernel(x_ref, pad_ref, g_ref, w_ref, qg_ref, kg_ref, cos_ref, sa_ref, sb_ref,
                          gmean_ref, qT_ref, k_ref, vT_ref, xr_ref, gate_ref, h0_ref, *, tm):
    j = pl.program_id(1)

    @pl.when(j == 0)
    def _():
        h0_ref[0:MAIN0, :] = pad_ref[...]
        h0_ref[MAIN0:tm, :] = x_ref[0, 0:tm - MAIN0, :]

    @pl.when(j > 0)
    def _():
        h0_ref[...] = x_ref[0]

    _in_proj_kernel(h0_ref, g_ref, w_ref, qg_ref, kg_ref, cos_ref, sa_ref, sb_ref, gmean_ref,
                    qT_ref, k_ref, vT_ref, xr_ref, gate_ref, tm=tm)


def _in_proj_kernel(h_ref, g_ref, w_ref, qg_ref, kg_ref, cos_ref, sa_ref, sb_ref, gmean_ref,
                    qT_ref, k_ref, vT_ref, xr_ref, gate_ref, *, tm):
    j = pl.program_id(1)
    x = h_ref[...]
    row = j * tm + lax.broadcasted_iota(jnp.int32, (tm, 1), 0)
    ms = jnp.mean(x * x, axis=-1, keepdims=True)
    hn = x * lax.rsqrt(ms + EPS) * g_ref[...]
    hn = jnp.where(row >= FRONT_PAD, hn, 0.0).astype(bf16)

    def proj(c0, width):
        return jnp.dot(hn, w_ref[:, c0:c0 + width], preferred_element_type=f32)

    cos, sa, sb = cos_ref[...], sa_ref[...], sb_ref[...]
    q = _qk_norm_rope(proj(0, D_ATTN), qg_ref[...], gmean_ref, cos, sa, sb,
                      ATTN_HEAD_DIM ** -0.5 * math.log2(math.e))
    qT_ref[...] = q.T.astype(bf16)
    k = _qk_norm_rope(proj(D_ATTN, D_ATTN), kg_ref[...], gmean_ref, cos, sa, sb, 1.0)
    k_ref[...] = k.astype(bf16)
    vT = proj(2 * D_ATTN, D_ATTN).T.astype(bf16)
    for hd in range(ATTN_HEADS):
        vT_ref[hd] = vT[hd * V_HEAD_DIM:(hd + 1) * V_HEAD_DIM, :]
    xr_ref[...] = proj(3 * D_ATTN, D_REC)
    gate_ref[...] = proj(3 * D_ATTN + D_REC, D_REC)


def _in_proj(h, g, w, layer, qg, kg, cos_t, sa_t, sb_t, gmean, x_pad=None):
    first = x_pad is not None
    B, D = h.shape[0], h.shape[2]
    T = h.shape[1] + (MAIN0 if first else 0)
    tm = ROW_TILE
    nt = T // tm
    row_blk = lambda width: pl.BlockSpec((None, tm, width), lambda b, j: (b, j, 0))
    tab_blk = pl.BlockSpec((tm, LANE), lambda b, j: (j, 0))
    if first:
        x_win = pl.BlockSpec(
            (pl.Element(1), pl.Element(tm), pl.Element(D)),
            lambda b, j: (b, pl.multiple_of(jnp.maximum(j * tm - MAIN0, 0), LANE), 0))
        lead_specs, lead_args = [x_win, _const_spec((MAIN0, D))], (h, x_pad)
        body = functools.partial(_in_proj_first_kernel, tm=tm)
    else:
        lead_specs, lead_args = [row_blk(D)], (h,)
        body = functools.partial(_in_proj_kernel, tm=tm)
    out_specs = [pl.BlockSpec((None, D_ATTN, tm), lambda b, j: (b, 0, j)),
                 row_blk(D_ATTN),
                 pl.BlockSpec((None, ATTN_HEADS, V_HEAD_DIM, tm), lambda b, j: (b, 0, 0, j)),
                 row_blk(D_REC), row_blk(D_REC)]
    out_shape = [jax.ShapeDtypeStruct((B, D_ATTN, T), bf16),
                 jax.ShapeDtypeStruct((B, T, D_ATTN), bf16),
                 jax.ShapeDtypeStruct((B, ATTN_HEADS, V_HEAD_DIM, T), bf16),
                 jax.ShapeDtypeStruct((B, T, D_REC), f32),
                 jax.ShapeDtypeStruct((B, T, D_REC), f32)]
    if first:
        out_specs.append(row_blk(D))
        out_shape.append(jax.ShapeDtypeStruct((B, T, D), f32))
    return pl.pallas_call(
        body,
        grid=(B, nt),
        in_specs=lead_specs + [_const_spec((1, D)), _const_spec((D, N_IN), layer),
                               _const_spec((1, D_ATTN)), _const_spec((1, D_ATTN)),
                               tab_blk, tab_blk, tab_blk,
                               _const_spec((D_ATTN // 2, D_ATTN // 2))],
        out_specs=out_specs,
        out_shape=out_shape,
        compiler_params=pltpu.CompilerParams(
            dimension_semantics=("parallel", "parallel"), vmem_limit_bytes=VMEM_LIMIT),
        name="in_proj",
    )(*lead_args, g, w, qg, kg, cos_t, sa_t, sb_t, gmean)


def _attn_kernel(flag_ref, bound_ref, qT_ref, k_ref, vT_ref, lq1_ref, lk1_ref, lq2_ref,
                 lk2_ref, g_ref, o_ref,
                 q2_ref, acc_ref, l_ref, m_ref, sa_ref, sb_ref, bias_ref, pa_ref, pb_ref, *, tile, big,
                 n_main, lam_init):
    lam = (jnp.exp(jnp.sum(lq1_ref[...] * lk1_ref[...], keepdims=True))
           - jnp.exp(jnp.sum(lq2_ref[...] * lk2_ref[...], keepdims=True)) + lam_init)
    hd = ATTN_HEAD_DIM

    def load_q(qs, w):
        qt = qT_ref[:, pl.ds(qs, w)]
        zero = jnp.zeros((hd, w), bf16)
        q2_ref[0:hd, 0:w] = qt[0:hd]
        q2_ref[hd:2 * hd, 0:w] = zero
        q2_ref[0:hd, w:2 * w] = zero
        q2_ref[hd:2 * hd, w:2 * w] = qt[hd:2 * hd]

    def colsum8(p):
        return jnp.sum(p.reshape(p.shape[0] // SUBLANE, SUBLANE, p.shape[1]), axis=0)

    def finish_tile(qs, w):
        denom = jnp.sum(l_ref[:, 0:2 * w], axis=0, keepdims=True)
        on = acc_ref[:, 0:2 * w] * (1.0 / denom)
        o = on[:, 0:w] - lam * on[:, w:2 * w]
        ms = jnp.mean(o * o, axis=0, keepdims=True)
        y = o * lax.rsqrt(ms + EPS) * g_ref[...] * (1.0 - lam_init)
        o_ref[pl.ds(qs, w), :] = y.T.astype(bf16)

    w0 = LANE
    load_q(0, w0)
    s0 = jnp.dot(k_ref[0:w0, :], q2_ref[:, 0:2 * w0], preferred_element_type=f32)
    kidx = lax.broadcasted_iota(jnp.int32, (w0, 2 * w0), 0)
    col = lax.broadcasted_iota(jnp.int32, (w0, 2 * w0), 1)
    qidx = jnp.where(col >= w0, col - w0, col)
    s0 = jnp.where((kidx >= FRONT_PAD) & (kidx <= qidx), s0, NEG)
    p0 = jnp.exp2(s0 - jnp.max(s0, axis=0, keepdims=True))
    l_ref[:, 0:2 * w0] = colsum8(p0)
    acc_ref[:, 0:2 * w0] = jnp.dot(vT_ref[:, 0:w0], p0.astype(bf16), preferred_element_type=f32)
    finish_tile(0, w0)

    kk = lax.broadcasted_iota(jnp.int32, (tile, tile), 0)
    cc = lax.broadcasted_iota(jnp.int32, (tile, tile), 1)
    bias_ref[...] = jnp.where(kk <= cc, 0.0, NEG)

    def tri_bias(cs):
        c0 = cs.start % tile
        return bias_ref[:, c0:c0 + (cs.stop - cs.start)]

    def meta_keys():
        sm = jnp.dot(k_ref[FRONT_PAD:MAIN0, :], q2_ref[:, 0:2 * tile],
                     preferred_element_type=f32)
        m0 = jnp.max(sm, axis=0, keepdims=True)
        pm = jnp.exp2(sm - m0)
        l_ref[:, 0:2 * tile] = colsum8(pm)
        pm = jnp.concatenate([jnp.zeros((FRONT_PAD, 2 * tile), bf16), pm.astype(bf16)], axis=0)
        acc_ref[:, 0:2 * tile] = jnp.dot(vT_ref[:, 0:MAIN0], pm, preferred_element_type=f32)
        m_ref[...] = m0

    def scores(ks, s_ref):
        s_ref[...] = jnp.dot(k_ref[pl.ds(ks, tile), :], q2_ref[:, 0:2 * tile],
                             preferred_element_type=f32)

    def softmax_pv(ks, s_ref, diagonal):
        vc = vT_ref[:, pl.ds(ks, tile)]
        for c0 in range(0, 2 * tile, ATTN_COL_BLOCK):
            cs = slice(c0, c0 + ATTN_COL_BLOCK)
            sc = s_ref[:, cs]
            if diagonal:
                sc = sc + tri_bias(cs)
            m_prev = m_ref[:, cs]
            m_new = jnp.maximum(m_prev, jnp.max(sc, axis=0, keepdims=True))
            alpha = jnp.exp2(m_prev - m_new)
            pc = jnp.exp2(sc - m_new)
            l_ref[:, cs] = alpha * l_ref[:, cs] + colsum8(pc)
            acc_ref[:, cs] = alpha * acc_ref[:, cs] + jnp.dot(vc, pc.astype(bf16),
                                                               preferred_element_type=f32)
            m_ref[:, cs] = m_new

    bound = bound_ref[...]
    n_big = n_main * tile // big

    def sub_blocks(r):
        return [slice(c * big + r * tile + c0, c * big + r * tile + c0 + ATTN_COL_BLOCK)
                for c in range(2) for c0 in range(0, tile, ATTN_COL_BLOCK)]

    all_blocks = sub_blocks(0) + sub_blocks(1)

    def stage_p(kc, cs, p_ref, diagonal):
        sc = jnp.dot(kc, q2_ref[:, cs], preferred_element_type=f32)
        if diagonal:
            sc = sc + tri_bias(cs)
        pc = jnp.exp2(sc - bound)
        l_ref[:, cs] += colsum8(pc)
        p_ref[:, cs] = pc.astype(bf16)

    def stage_pv(vc, cs, p_ref):
        acc_ref[:, cs] += jnp.dot(vc, p_ref[:, cs], preferred_element_type=f32)

    def bounded_start(ti):
        qs = pl.multiple_of(MAIN0 + ti * big, LANE)
        load_q(qs, big)
        sm = jnp.dot(k_ref[FRONT_PAD:MAIN0, :], q2_ref[...], preferred_element_type=f32)
        pm = jnp.exp2(sm - bound)
        l_ref[...] = colsum8(pm)
        pm = jnp.concatenate([jnp.zeros((FRONT_PAD, 2 * big), bf16), pm.astype(bf16)], axis=0)
        acc_ref[...] = jnp.dot(vT_ref[:, 0:MAIN0], pm, preferred_element_type=f32)
        kc = k_ref[pl.ds(qs, tile), :]
        for cs in sub_blocks(0):
            stage_p(kc, cs, pa_ref, True)
        for cs in sub_blocks(1):
            stage_p(kc, cs, pa_ref, False)

    def bounded_tile(ti, carry):
        qs = pl.multiple_of(MAIN0 + ti * big, LANE)

        def full_start(j):
            return pl.multiple_of(MAIN0 + j * tile, LANE)

        def step(v_start, k_start, cur_ref, next_ref):
            vc = vT_ref[:, pl.ds(v_start, tile)]
            kc = k_ref[pl.ds(k_start, tile), :]
            for cs in all_blocks:
                stage_pv(vc, cs, cur_ref)
                stage_p(kc, cs, next_ref, False)

        def pair(t, c):
            held = pl.multiple_of(jnp.where(t == 0, qs, MAIN0 + (2 * t - 1) * tile), LANE)
            step(held, full_start(2 * t), pa_ref, pb_ref)
            step(full_start(2 * t), full_start(2 * t + 1), pb_ref, pa_ref)
            return c

        lax.fori_loop(0, ti, pair, 0)

        held = pl.multiple_of(jnp.where(ti == 0, qs, MAIN0 + (2 * ti - 1) * tile), LANE)
        vc = vT_ref[:, pl.ds(held, tile)]
        kd = k_ref[pl.ds(qs + tile, tile), :]
        for cs in sub_blocks(0):
            stage_pv(vc, cs, pa_ref)
        for cs in sub_blocks(1):
            stage_pv(vc, cs, pa_ref)
            stage_p(kd, cs, pb_ref, True)
        vd = vT_ref[:, pl.ds(qs + tile, tile)]
        for cs in sub_blocks(1):
            stage_pv(vd, cs, pb_ref)
        finish_tile(qs, big)
        bounded_start(jnp.minimum(ti + 1, n_big - 1))
        return carry

    def online_tile(qi, carry):
        qs = pl.multiple_of(MAIN0 + qi * tile, LANE)
        load_q(qs, tile)
        meta_keys()
        scores(MAIN0, sa_ref)

        def step(ks, cur_ref, next_ref):
            scores(pl.multiple_of(ks + tile, LANE), next_ref)
            softmax_pv(ks, cur_ref, False)

        def pair(t, c):
            ks = pl.multiple_of(MAIN0 + 2 * t * tile, LANE)
            step(ks, sa_ref, sb_ref)
            step(pl.multiple_of(ks + tile, LANE), sb_ref, sa_ref)
            return c

        lax.fori_loop(0, qi // 2, pair, 0)
        odd = qi % 2 == 1

        @pl.when(odd)
        def _():
            step(pl.multiple_of(qs - tile, LANE), sa_ref, sb_ref)
            softmax_pv(qs, sb_ref, True)

        @pl.when(jnp.logical_not(odd))
        def _():
            softmax_pv(qs, sa_ref, True)

        finish_tile(qs, tile)
        return carry

    use_bounded = flag_ref[0] == 1

    @pl.when(use_bounded)
    def _():
        bounded_start(0)
        lax.fori_loop(0, n_big, bounded_tile, 0)

    @pl.when(jnp.logical_not(use_bounded))
    def _():
        lax.fori_loop(0, n_main, online_tile, 0)


def _attention(qT, k, vT, lq1, lk1, lq2, lk2, g_col, bound, lam_init):
    B, _, T = qT.shape
    tile, big = ATTN_TILE, 2 * ATTN_TILE
    assert (T - MAIN0) % big == 0
    n_main = (T - MAIN0) // tile
    vec = _const_spec((1, ATTN_HEAD_DIM))
    flag = (bound <= MAX_SCORE_BOUND).astype(jnp.int32).reshape(1)
    return pl.pallas_call(
        functools.partial(_attn_kernel, tile=tile, big=big, n_main=n_main, lam_init=lam_init),
        grid=(B, ATTN_HEADS),
        in_specs=[pl.BlockSpec(memory_space=pltpu.SMEM), _const_spec((1, 1)),
                  pl.BlockSpec((None, V_HEAD_DIM, T), lambda b, h: (b, h, 0)),
                  pl.BlockSpec((None, T, V_HEAD_DIM), lambda b, h: (b, 0, h)),
                  pl.BlockSpec((None, None, V_HEAD_DIM, T), lambda b, h: (b, h, 0, 0)),
                  vec, vec, vec, vec, _const_spec((V_HEAD_DIM, 1))],
        out_specs=pl.BlockSpec((None, T, V_HEAD_DIM), lambda b, h: (b, 0, h)),
        out_shape=jax.ShapeDtypeStruct((B, T, D_ATTN), bf16),
        scratch_shapes=[pltpu.VMEM((V_HEAD_DIM, 2 * big), bf16),
                        pltpu.VMEM((V_HEAD_DIM, 2 * big), f32),
                        pltpu.VMEM((SUBLANE, 2 * big), f32),
                        pltpu.VMEM((1, 2 * tile), f32),
                        pltpu.VMEM((tile, 2 * tile), f32),
                        pltpu.VMEM((tile, 2 * tile), f32),
                        pltpu.VMEM((tile, tile), f32),
                        pltpu.VMEM((tile, 2 * big), bf16),
                        pltpu.VMEM((tile, 2 * big), bf16)],
        compiler_params=pltpu.CompilerParams(
            dimension_semantics=("parallel", "parallel"), vmem_limit_bytes=VMEM_LIMIT),
        name="diff_attn",
    )(flag, bound.reshape(1, 1), qT, k, vT, lq1, lk1, lq2, lk2, g_col)


def _rglru_kernel(xr_ref, gate_ref, cw_ref, cb_ref, wr_ref, wi_ref, brg_ref, big_ref, L_ref,
                  g_ref, o_ref, xtail_ref, h_ref, ga_ref, gu_ref, hs_ref, *, tc):
    j = pl.program_id(1)

    @pl.when(j == 0)
    def _():
        xtail_ref[...] = jnp.zeros((SUBLANE, D_REC), f32)
        h_ref[...] = jnp.zeros((1, D_REC), f32)

    x = xr_ref[...]
    ng = tc // SUBLANE
    x3 = x.reshape(ng, SUBLANE, D_REC)
    prev3 = xtail_ref[...].reshape(1, SUBLANE, D_REC)
    sub3 = lax.broadcasted_iota(jnp.int32, (1, SUBLANE, 1), 1)
    xc3 = cw_ref[CONV_WIDTH - 1:CONV_WIDTH, :] * x3
    for k in range(1, CONV_WIDTH):
        rot = pltpu.roll(jnp.concatenate([prev3, x3], axis=0), k, 1)
        shifted = jnp.where(sub3 < k, rot[0:ng], rot[1:ng + 1])
        xc3 = xc3 + cw_ref[CONV_WIDTH - 1 - k:CONV_WIDTH - k, :] * shifted
    xc = xc3.reshape(tc, D_REC) + cb_ref[...]
    xtail_ref[...] = x[tc - SUBLANE:tc, :]

    xcb = xc.astype(bf16)
    half = D_REC // 2

    def gate_lin(w_ref, b_ref):
        lo = jnp.dot(xcb[:, 0:half], w_ref[0], preferred_element_type=f32)
        hi = jnp.dot(xcb[:, half:D_REC], w_ref[1], preferred_element_type=f32)
        return jnp.concatenate([lo, hi], axis=1) + b_ref[...]

    r = jax.nn.sigmoid(gate_lin(wr_ref, brg_ref))
    i = jax.nn.sigmoid(gate_lin(wi_ref, big_ref))
    log_a = LRU_C * r * jax.nn.log_sigmoid(L_ref[...])
    a = jnp.exp(log_a)
    mult = jnp.sqrt(jnp.maximum(-jnp.tanh(log_a) * (a * a + 1.0), 0.0))
    row = lax.broadcasted_iota(jnp.int32, (tc, 1), 0)
    t_idx = j * tc + row
    mult = jnp.where(t_idx == FRONT_PAD, 1.0, mult)
    u = jnp.where(t_idx >= FRONT_PAD, mult * i * xc, 0.0)

    a3 = a.reshape(ng, SUBLANE, D_REC)
    u3 = u.reshape(ng, SUBLANE, D_REC)
    d = 1
    while d < SUBLANE:
        keep = sub3 >= d
        a_s = jnp.where(keep, pltpu.roll(a3, d, 1), 1.0)
        u_s = jnp.where(keep, pltpu.roll(u3, d, 1), 0.0)
        u3 = a3 * u_s + u3
        a3 = a3 * a_s
        d *= 2
    ga_ref[...] = a3.reshape(tc, D_REC)
    gu_ref[...] = u3.reshape(tc, D_REC)

    def carry_group(gi, h):
        r0 = pl.multiple_of(gi * SUBLANE, SUBLANE)
        hg = gu_ref[pl.ds(r0, SUBLANE), :] + ga_ref[pl.ds(r0, SUBLANE), :] * h
        hs_ref[pl.ds(r0, SUBLANE), :] = hg
        return hg[SUBLANE - 1:SUBLANE, :]

    h_ref[...] = lax.fori_loop(0, tc // SUBLANE, carry_group, h_ref[...], unroll=SCAN_UNROLL)
    hseq = hs_ref[...]

    y = hseq * jax.nn.gelu(gate_ref[...])
    ms = jnp.mean(y * y, axis=-1, keepdims=True)
    o_ref[...] = (y * lax.rsqrt(ms + EPS) * g_ref[...]).astype(bf16)


def _rglru(xr, gate, cw, cb, wr, wi, brg, big, L, g):
    B, T, _ = xr.shape
    tc = ROW_TILE
    row_blk = pl.BlockSpec((None, tc, D_REC), lambda b, j: (b, j, 0))
    vec = _const_spec((1, D_REC))
    half = D_REC // 2
    return pl.pallas_call(
        functools.partial(_rglru_kernel, tc=tc),
        grid=(B, T // tc),
        in_specs=[row_blk, row_blk, _const_spec((CONV_WIDTH, D_REC)), vec,
                  _const_spec((2, half, half)), _const_spec((2, half, half)),
                  vec, vec, vec, vec],
        out_specs=row_blk,
        out_shape=jax.ShapeDtypeStruct((B, T, D_REC), bf16),
        scratch_shapes=[pltpu.VMEM((SUBLANE, D_REC), f32),
                        pltpu.VMEM((1, D_REC), f32),
                        pltpu.VMEM((tc, D_REC), f32),
                        pltpu.VMEM((tc, D_REC), f32),
                        pltpu.VMEM((tc, D_REC), f32)],
        compiler_params=pltpu.CompilerParams(
            dimension_semantics=("parallel", "arbitrary"), vmem_limit_bytes=VMEM_LIMIT),
        name="rglru",
    )(xr, gate, cw, cb, wr, wi, brg, big, L, g)


def _out_ffn_kernel(h_ref, oa_ref, or_ref, wo_ref, g_ref, wgu_ref, wd_ref, out_ref, *,
                    leading_unit_dim):
    if leading_unit_dim:
        h_ref, oa_ref, or_ref = h_ref.at[0], oa_ref.at[0], or_ref.at[0]
    out_ref[...] = (h_ref[...]
                    + jnp.dot(oa_ref[...], wo_ref[0:D_ATTN, :], preferred_element_type=f32)
                    + jnp.dot(or_ref[...], wo_ref[D_ATTN:D_ATTN + D_REC, :],
                              preferred_element_type=f32))
    h1 = out_ref[...]
    ms = jnp.mean(h1 * h1, axis=-1, keepdims=True)
    hn = (h1 * lax.rsqrt(ms + EPS) * g_ref[...]).astype(bf16)
    ffn = None
    for c0 in range(0, D_FF, FFN_CHUNK):
        gt = jnp.dot(hn, wgu_ref[:, c0:c0 + FFN_CHUNK], preferred_element_type=f32)
        up = jnp.dot(hn, wgu_ref[:, D_FF + c0:D_FF + c0 + FFN_CHUNK],
                     preferred_element_type=f32)
        act = (jax.nn.silu(gt) * up).astype(bf16)
        part = jnp.dot(act, wd_ref[c0:c0 + FFN_CHUNK, :], preferred_element_type=f32)
        ffn = part if ffn is None else ffn + part
    out_ref[...] += ffn


def _ffn_weight_specs(layer):
    return [_const_spec((D_ATTN + D_REC, D_MODEL), layer), _const_spec((1, D_MODEL)),
            _const_spec((D_MODEL, 2 * D_FF), layer), _const_spec((D_FF, D_MODEL), layer)]


def _out_ffn(h, oa, orec, wo, g, wgu, wd, layer):
    R, D = h.shape
    tm = FFN_ROW_TILE
    row_blk = lambda width: pl.BlockSpec((tm, width), lambda i: (i, 0))
    return pl.pallas_call(
        functools.partial(_out_ffn_kernel, leading_unit_dim=False),
        grid=(R // tm,),
        in_specs=[row_blk(D), row_blk(D_ATTN), row_blk(D_REC)] + _ffn_weight_specs(layer),
        out_specs=row_blk(D),
        out_shape=jax.ShapeDtypeStruct((R, D), f32),
        compiler_params=pltpu.CompilerParams(
            dimension_semantics=("parallel",), vmem_limit_bytes=VMEM_LIMIT),
        name="out_ffn",
    )(h, oa, orec, wo, g, wgu, wd)


def _out_ffn_last(h, oa, orec, wo, g, wgu, wd, layer):
    B, T, D = h.shape
    S = T - MAIN0
    tm = FFN_ROW_TILE

    def win(width):
        return pl.BlockSpec((pl.Element(1), pl.Element(tm), pl.Element(width)),
                            lambda b, i: (b, pl.multiple_of(MAIN0 + i * tm, LANE), 0))

    return pl.pallas_call(
        functools.partial(_out_ffn_kernel, leading_unit_dim=True),
        grid=(B, S // tm),
        in_specs=[win(D), win(D_ATTN), win(D_REC)] + _ffn_weight_specs(layer),
        out_specs=pl.BlockSpec((None, tm, D), lambda b, i: (b, i, 0)),
        out_shape=jax.ShapeDtypeStruct((B, S, D), f32),
        compiler_params=pltpu.CompilerParams(
            dimension_semantics=("parallel", "parallel"), vmem_limit_bytes=VMEM_LIMIT),
        name="out_ffn_last",
    )(h, oa, orec, wo, g, wgu, wd)


def _rope_tables(T):
    half = ROPE_DIM // 2
    inv = ROPE_THETA ** (-jnp.arange(0, ROPE_DIM, 2, dtype=f32) / ROPE_DIM)
    pos = (jnp.arange(T) - FRONT_PAD).astype(f32)
    ang = pos[:, None] * inv[None, :]
    d = jnp.arange(LANE) % ATTN_HEAD_DIM
    first, second = d < half, (d >= half) & (d < ROPE_DIM)
    cos_l = jnp.cos(ang)[:, d % half]
    sin_l = jnp.sin(ang)[:, d % half]
    cos_t = jnp.where(first | second, cos_l, 1.0)
    sin_a = jnp.where(first, -sin_l, 0.0)
    sin_b = jnp.where(second, sin_l, 0.0)
    return cos_t, sin_a, sin_b


def _block_diag_halves(w):
    nb = REC_BLOCKS // 2
    w4 = w.reshape(2, nb, REC_BLOCK_DIM, REC_BLOCK_DIM)
    eye = jnp.eye(nb, dtype=w.dtype)
    half = D_REC // 2
    return jnp.einsum('hnde,nm->hndme', w4, eye).reshape(2, half, half)


def kernel(x, meta_tokens, norm_mix_g, w_in, q_norm_g, k_norm_g, lambda_q1, lambda_k1,
           lambda_q2, lambda_k2, subln_g, conv_w, conv_b, w_rg, b_rg, w_ig, b_ig, lru_L,
           rec_norm_g, w_out, norm_ffn_g, w_gu, w_down):
    B, S, D = x.shape
    assert D == D_MODEL and S % ATTN_TILE == 0
    T = S + LANE
    assert T % ROW_TILE == 0 and (B * T) % FFN_ROW_TILE == 0

    x_pad = jnp.concatenate([jnp.zeros((FRONT_PAD, D), x.dtype), meta_tokens.astype(x.dtype)], 0)
    cos_t, sin_a, sin_b = _rope_tables(T)
    grp = jnp.arange(D_ATTN // 2) // ATTN_HEAD_DIM
    gmean = jnp.where(grp[:, None] == grp[None, :], 1.0 / ATTN_HEAD_DIM, 0.0).astype(bf16)
    n_grp = D_ATTN // ATTN_HEAD_DIM
    row = lambda v: v.reshape(1, -1)
    w_in_b, w_out_b = w_in.astype(bf16), w_out.astype(bf16)
    w_gu_b, w_down_b = w_gu.astype(bf16), w_down.astype(bf16)

    h = x
    for l in range(DEPTH):
        lam_init = 0.8 - 0.6 * math.exp(-0.3 * l)
        outs = _in_proj(
            h, row(norm_mix_g[l]), w_in_b, l,
            row(jnp.tile(q_norm_g[l], n_grp)), row(jnp.tile(k_norm_g[l], n_grp)),
            cos_t, sin_a, sin_b, gmean, x_pad=x_pad if l == 0 else None)
        qT, k, vT, xr, gate = outs[:5]
        if l == 0:
            h = outs[5]
        bound = (ATTN_HEAD_DIM ** 0.5 * math.log2(math.e)
                 * jnp.max(jnp.abs(q_norm_g[l])) * jnp.max(jnp.abs(k_norm_g[l])))
        o_attn = _attention(qT, k, vT, row(lambda_q1[l]), row(lambda_k1[l]),
                            row(lambda_q2[l]), row(lambda_k2[l]),
                            subln_g[l].reshape(V_HEAD_DIM, 1), bound, lam_init)
        o_rec = _rglru(xr, gate, conv_w[l], row(conv_b[l]),
                       _block_diag_halves(w_rg[l]).astype(bf16),
                       _block_diag_halves(w_ig[l]).astype(bf16),
                       row(b_rg[l]), row(b_ig[l]), row(lru_L[l]), row(rec_norm_g[l]))
        ffn_w = (w_out_b, row(norm_ffn_g[l]), w_gu_b, w_down_b, l)
        if l == DEPTH - 1:
            return _out_ffn_last(h, o_attn, o_rec, *ffn_w)
        h = _out_ffn(h.reshape(B * T, D), o_attn.reshape(B * T, D_ATTN),
                     o_rec.reshape(B * T, D_REC), *ffn_w).reshape(B, T, D)
```

```python
import functools
import math

import jax
import jax.numpy as jnp
from jax import lax
from jax.experimental import pallas as pl
from jax.experimental.pallas import tpu as pltpu

D_MODEL = 1024
DEPTH = 2
N_META = 16
D_ATTN = 512
D_REC = 512
ATTN_HEADS = 4
ATTN_HEAD_DIM = 64
V_HEAD_DIM = 2 * ATTN_HEAD_DIM
ROPE_DIM = ATTN_HEAD_DIM // 4
ROPE_THETA = 500000.0
REC_BLOCKS = 8
REC_BLOCK_DIM = D_REC // REC_BLOCKS
CONV_WIDTH = 4
LRU_C = 8.0
D_FF = 2816
EPS = 1e-6
N_IN = 3 * D_ATTN + 2 * D_REC

LANE = 128
SUBLANE = 8
FRONT_PAD = LANE - N_META
MAIN0 = LANE
NEG = -1e30
MAX_SCORE_BOUND = 50.0
VMEM_LIMIT = 56 * 1024 * 1024

ROW_TILE = 640
SCAN_UNROLL = 8
FFN_ROW_TILE = 512
FFN_CHUNK = 256
ATTN_TILE = 512
ATTN_COL_BLOCK = 256

f32 = jnp.float32
bf16 = jnp.bfloat16


def _const_spec(shape, layer=None):
    zeros = (0,) * len(shape)
    if layer is None:
        return pl.BlockSpec(shape, lambda *_: zeros, pipeline_mode=pl.Buffered(1))
    return pl.BlockSpec((None,) + tuple(shape), lambda *_: (layer,) + zeros,
                        pipeline_mode=pl.Buffered(1))


def _qk_norm_rope(z, gain, gmean_ref, cos, sin_a, sin_b, scale):
    zz = (z * z).astype(bf16)
    half = D_ATTN // 2
    ms = jnp.concatenate(
        [jnp.dot(zz[:, 0:half], gmean_ref[...], preferred_element_type=f32),
         jnp.dot(zz[:, half:D_ATTN], gmean_ref[...], preferred_element_type=f32)], axis=1)
    zn = z * lax.rsqrt(ms + EPS) * gain
    if scale != 1.0:
        zn = zn * scale
    outs = []
    for c in range(D_ATTN // LANE):
        zc = zn[:, c * LANE:(c + 1) * LANE]
        half = ROPE_DIM // 2
        outs.append(zc * cos
                    + pltpu.roll(zc, LANE - half, 1) * sin_a
                    + pltpu.roll(zc, half, 1) * sin_b)
    return jnp.concatenate(outs, axis=1)


def _in_proj_first_kernel(x_ref, pad_ref, *refs, tm):
    j = pl.program_id(1)
    h0_ref = refs[N_PROJ_IN - 1 + N_REC_IN + N_PROJ_OUT]

    @pl.when(j == 0)
    def _():
        h0_ref[0:MAIN0, :] = pad_ref[...]
        h0_ref[MAIN0:tm, :] = x_ref[0, 0:tm - MAIN0, :]

    @pl.when(j > 0)
    def _():
        h0_ref[...] = x_ref[0]

    _in_proj_kernel(h0_ref, *refs[:N_PROJ_IN - 1 + N_REC_IN + N_PROJ_OUT],
                    *refs[N_PROJ_IN - 1 + N_REC_IN + N_PROJ_OUT + 1:], tm=tm)


N_PROJ_IN = 9
N_REC_IN = 8
N_PROJ_OUT = 4


def _in_proj_kernel(h_ref, g_ref, w_ref, qg_ref, kg_ref, cos_ref, sa_ref, sb_ref, gmean_ref,
                    cw_ref, cb_ref, wr_ref, wi_ref, brg_ref, big_ref, L_ref, recg_ref,
                    qT_ref, k_ref, vT_ref, orec_ref,
                    xtail_ref, hstate_ref, ga_ref, gu_ref, hs_ref, *, tm):
    j = pl.program_id(1)
    x = h_ref[...]
    row = j * tm + lax.broadcasted_iota(jnp.int32, (tm, 1), 0)
    ms = jnp.mean(x * x, axis=-1, keepdims=True)
    hn = x * lax.rsqrt(ms + EPS) * g_ref[...]
    hn = jnp.where(row >= FRONT_PAD, hn, 0.0).astype(bf16)

    def proj(c0, width):
        return jnp.dot(hn, w_ref[:, c0:c0 + width], preferred_element_type=f32)

    cos, sa, sb = cos_ref[...], sa_ref[...], sb_ref[...]
    q = _qk_norm_rope(proj(0, D_ATTN), qg_ref[...], gmean_ref, cos, sa, sb,
                      ATTN_HEAD_DIM ** -0.5 * math.log2(math.e))
    qT_ref[...] = q.T.astype(bf16)
    k = _qk_norm_rope(proj(D_ATTN, D_ATTN), kg_ref[...], gmean_ref, cos, sa, sb, 1.0)
    k_ref[...] = k.astype(bf16)
    vT = proj(2 * D_ATTN, D_ATTN).T.astype(bf16)
    for hd in range(ATTN_HEADS):
        vT_ref[hd] = vT[hd * V_HEAD_DIM:(hd + 1) * V_HEAD_DIM, :]
    _rglru_scan(proj(3 * D_ATTN, D_REC), j, cw_ref, cb_ref, wr_ref, wi_ref, brg_ref, big_ref,
                L_ref, xtail_ref, hstate_ref, ga_ref, gu_ref, hs_ref, tc=tm)
    _rglru_finish(proj(3 * D_ATTN + D_REC, D_REC), recg_ref, orec_ref, hs_ref)


def _in_proj(h, g, w, layer, qg, kg, cos_t, sa_t, sb_t, gmean, rec, x_pad=None):
    first = x_pad is not None
    B, D = h.shape[0], h.shape[2]
    T = h.shape[1] + (MAIN0 if first else 0)
    tm = ROW_TILE
    nt = T // tm
    row_blk = lambda width: pl.BlockSpec((None, tm, width), lambda b, j: (b, j, 0))
    tab_blk = pl.BlockSpec((tm, LANE), lambda b, j: (j, 0))
    vec_rec = _const_spec((1, D_REC))
    half = D_REC // 2
    if first:
        x_win = pl.BlockSpec(
            (pl.Element(1), pl.Element(tm), pl.Element(D)),
            lambda b, j: (b, pl.multiple_of(jnp.maximum(j * tm - MAIN0, 0), LANE), 0))
        lead_specs, lead_args = [x_win, _const_spec((MAIN0, D))], (h, x_pad)
        body = functools.partial(_in_proj_first_kernel, tm=tm)
    else:
        lead_specs, lead_args = [row_blk(D)], (h,)
        body = functools.partial(_in_proj_kernel, tm=tm)
    out_specs = [pl.BlockSpec((None, D_ATTN, tm), lambda b, j: (b, 0, j)),
                 row_blk(D_ATTN),
                 pl.BlockSpec((None, ATTN_HEADS, V_HEAD_DIM, tm), lambda b, j: (b, 0, 0, j)),
                 row_blk(D_REC)]
    out_shape = [jax.ShapeDtypeStruct((B, D_ATTN, T), bf16),
                 jax.ShapeDtypeStruct((B, T, D_ATTN), bf16),
                 jax.ShapeDtypeStruct((B, ATTN_HEADS, V_HEAD_DIM, T), bf16),
                 jax.ShapeDtypeStruct((B, T, D_REC), bf16)]
    if first:
        out_specs.append(row_blk(D))
        out_shape.append(jax.ShapeDtypeStruct((B, T, D), f32))
    return pl.pallas_call(
        body,
        grid=(B, nt),
        in_specs=lead_specs + [_const_spec((1, D)), _const_spec((D, N_IN), layer),
                               _const_spec((1, D_ATTN)), _const_spec((1, D_ATTN)),
                               tab_blk, tab_blk, tab_blk,
                               _const_spec((D_ATTN // 2, D_ATTN // 2)),
                               _const_spec((CONV_WIDTH, D_REC)), vec_rec,
                               _const_spec((2, half, half)), _const_spec((2, half, half)),
                               vec_rec, vec_rec, vec_rec, vec_rec],
        out_specs=out_specs,
        out_shape=out_shape,
        scratch_shapes=[pltpu.VMEM((SUBLANE, D_REC), f32),
                        pltpu.VMEM((1, D_REC), f32),
                        pltpu.VMEM((tm, D_REC), f32),
                        pltpu.VMEM((tm, D_REC), f32),
                        pltpu.VMEM((tm, D_REC), f32)],
        compiler_params=pltpu.CompilerParams(
            dimension_semantics=("parallel", "arbitrary"), vmem_limit_bytes=VMEM_LIMIT),
        name="in_proj",
    )(*lead_args, g, w, qg, kg, cos_t, sa_t, sb_t, gmean, *rec)


def _attn_kernel(flag_ref, bound_ref, qT_ref, k_ref, vT_ref, lq1_ref, lk1_ref, lq2_ref,
                 lk2_ref, g_ref, o_ref,
                 q2_ref, acc_ref, l_ref, m_ref, sa_ref, sb_ref, bias_ref, pa_ref, pb_ref, *, tile, big,
                 n_main, lam_init):
    lam = (jnp.exp(jnp.sum(lq1_ref[...] * lk1_ref[...], keepdims=True))
           - jnp.exp(jnp.sum(lq2_ref[...] * lk2_ref[...], keepdims=True)) + lam_init)
    hd = ATTN_HEAD_DIM

    def load_q(qs, w):
        qt = qT_ref[:, pl.ds(qs, w)]
        zero = jnp.zeros((hd, w), bf16)
        q2_ref[0:hd, 0:w] = qt[0:hd]
        q2_ref[hd:2 * hd, 0:w] = zero
        q2_ref[0:hd, w:2 * w] = zero
        q2_ref[hd:2 * hd, w:2 * w] = qt[hd:2 * hd]

    def colsum8(p):
        return jnp.sum(p.reshape(p.shape[0] // SUBLANE, SUBLANE, p.shape[1]), axis=0)

    def finish_tile(qs, w):
        denom = jnp.sum(l_ref[:, 0:2 * w], axis=0, keepdims=True)
        on = acc_ref[:, 0:2 * w] * (1.0 / denom)
        o = on[:, 0:w] - lam * on[:, w:2 * w]
        ms = jnp.mean(o * o, axis=0, keepdims=True)
        y = o * lax.rsqrt(ms + EPS) * g_ref[...] * (1.0 - lam_init)
        o_ref[pl.ds(qs, w), :] = y.T.astype(bf16)

    w0 = LANE
    load_q(0, w0)
    s0 = jnp.dot(k_ref[0:w0, :], q2_ref[:, 0:2 * w0], preferred_element_type=f32)
    kidx = lax.broadcasted_iota(jnp.int32, (w0, 2 * w0), 0)
    col = lax.broadcasted_iota(jnp.int32, (w0, 2 * w0), 1)
    qidx = jnp.where(col >= w0, col - w0, col)
    s0 = jnp.where((kidx >= FRONT_PAD) & (kidx <= qidx), s0, NEG)
    p0 = jnp.exp2(s0 - jnp.max(s0, axis=0, keepdims=True))
    l_ref[:, 0:2 * w0] = colsum8(p0)
    acc_ref[:, 0:2 * w0] = jnp.dot(vT_ref[:, 0:w0], p0.astype(bf16), preferred_element_type=f32)
    finish_tile(0, w0)

    kk = lax.broadcasted_iota(jnp.int32, (tile, tile), 0)
    cc = lax.broadcasted_iota(jnp.int32, (tile, tile), 1)
    bias_ref[...] = jnp.where(kk <= cc, 0.0, NEG)

    def tri_bias(cs):
        c0 = cs.start % tile
        return bias_ref[:, c0:c0 + (cs.stop - cs.start)]

    def meta_keys():
        sm = jnp.dot(k_ref[FRONT_PAD:MAIN0, :], q2_ref[:, 0:2 * tile],
                     preferred_element_type=f32)
        m0 = jnp.max(sm, axis=0, keepdims=True)
        pm = jnp.exp2(sm - m0)
        l_ref[:, 0:2 * tile] = colsum8(pm)
        pm = jnp.concatenate([jnp.zeros((FRONT_PAD, 2 * tile), bf16), pm.astype(bf16)], axis=0)
        acc_ref[:, 0:2 * tile] = jnp.dot(vT_ref[:, 0:MAIN0], pm, preferred_element_type=f32)
        m_ref[...] = m0

    def scores(ks, s_ref):
        s_ref[...] = jnp.dot(k_ref[pl.ds(ks, tile), :], q2_ref[:, 0:2 * tile],
                             preferred_element_type=f32)

    def softmax_pv(ks, s_ref, diagonal):
        vc = vT_ref[:, pl.ds(ks, tile)]
        for c0 in range(0, 2 * tile, ATTN_COL_BLOCK):
            cs = slice(c0, c0 + ATTN_COL_BLOCK)
            sc = s_ref[:, cs]
            if diagonal:
                sc = sc + tri_bias(cs)
            m_prev = m_ref[:, cs]
            m_new = jnp.maximum(m_prev, jnp.max(sc, axis=0, keepdims=True))
            alpha = jnp.exp2(m_prev - m_new)
            pc = jnp.exp2(sc - m_new)
            l_ref[:, cs] = alpha * l_ref[:, cs] + colsum8(pc)
            acc_ref[:, cs] = alpha * acc_ref[:, cs] + jnp.dot(vc, pc.astype(bf16),
                                                               preferred_element_type=f32)
            m_ref[:, cs] = m_new

    bound = bound_ref[...]
    n_big = n_main * tile // big

    def sub_blocks(r):
        return [slice(c * big + r * tile + c0, c * big + r * tile + c0 + ATTN_COL_BLOCK)
                for c in range(2) for c0 in range(0, tile, ATTN_COL_BLOCK)]

    all_blocks = sub_blocks(0) + sub_blocks(1)

    def stage_p(kc, cs, p_ref, diagonal):
        sc = jnp.dot(kc, q2_ref[:, cs], preferred_element_type=f32)
        if diagonal:
            sc = sc + tri_bias(cs)
        pc = jnp.exp2(sc - bound)
        l_ref[:, cs] += colsum8(pc)
        p_ref[:, cs] = pc.astype(bf16)

    def stage_pv(vc, cs, p_ref):
        acc_ref[:, cs] += jnp.dot(vc, p_ref[:, cs], preferred_element_type=f32)

    def bounded_start(ti):
        qs = pl.multiple_of(MAIN0 + ti * big, LANE)
        load_q(qs, big)
        sm = jnp.dot(k_ref[FRONT_PAD:MAIN0, :], q2_ref[...], preferred_element_type=f32)
        pm = jnp.exp2(sm - bound)
        l_ref[...] = colsum8(pm)
        pm = jnp.concatenate([jnp.zeros((FRONT_PAD, 2 * big), bf16), pm.astype(bf16)], axis=0)
        acc_ref[...] = jnp.dot(vT_ref[:, 0:MAIN0], pm, preferred_element_type=f32)
        kc = k_ref[pl.ds(qs, tile), :]
        for cs in sub_blocks(0):
            stage_p(kc, cs, pa_ref, True)
        for cs in sub_blocks(1):
            stage_p(kc, cs, pa_ref, False)

    def bounded_tile(ti, carry):
        qs = pl.multiple_of(MAIN0 + ti * big, LANE)

        def full_start(j):
            return pl.multiple_of(MAIN0 + j * tile, LANE)

        def step(v_start, k_start, cur_ref, next_ref):
            vc = vT_ref[:, pl.ds(v_start, tile)]
            kc = k_ref[pl.ds(k_start, tile), :]
            for cs in all_blocks:
                stage_pv(vc, cs, cur_ref)
                stage_p(kc, cs, next_ref, False)

        def pair(t, c):
            held = pl.multiple_of(jnp.where(t == 0, qs, MAIN0 + (2 * t - 1) * tile), LANE)
            step(held, full_start(2 * t), pa_ref, pb_ref)
            step(full_start(2 * t), full_start(2 * t + 1), pb_ref, pa_ref)
            return c

        lax.fori_loop(0, ti, pair, 0)

        held = pl.multiple_of(jnp.where(ti == 0, qs, MAIN0 + (2 * ti - 1) * tile), LANE)
        vc = vT_ref[:, pl.ds(held, tile)]
        kd = k_ref[pl.ds(qs + tile, tile), :]
        for cs in sub_blocks(0):
            stage_pv(vc, cs, pa_ref)
        for cs in sub_blocks(1):
            stage_pv(vc, cs, pa_ref)
            stage_p(kd, cs, pb_ref, True)
        vd = vT_ref[:, pl.ds(qs + tile, tile)]
        for cs in sub_blocks(1):
            stage_pv(vd, cs, pb_ref)
        finish_tile(qs, big)
        bounded_start(jnp.minimum(ti + 1, n_big - 1))
        return carry

    def online_tile(qi, carry):
        qs = pl.multiple_of(MAIN0 + qi * tile, LANE)
        load_q(qs, tile)
        meta_keys()
        scores(MAIN0, sa_ref)

        def step(ks, cur_ref, next_ref):
            scores(pl.multiple_of(ks + tile, LANE), next_ref)
            softmax_pv(ks, cur_ref, False)

        def pair(t, c):
            ks = pl.multiple_of(MAIN0 + 2 * t * tile, LANE)
            step(ks, sa_ref, sb_ref)
            step(pl.multiple_of(ks + tile, LANE), sb_ref, sa_ref)
            return c

        lax.fori_loop(0, qi // 2, pair, 0)
        odd = qi % 2 == 1

        @pl.when(odd)
        def _():
            step(pl.multiple_of(qs - tile, LANE), sa_ref, sb_ref)
            softmax_pv(qs, sb_ref, True)

        @pl.when(jnp.logical_not(odd))
        def _():
            softmax_pv(qs, sa_ref, True)

        finish_tile(qs, tile)
        return carry

    use_bounded = flag_ref[0] == 1

    @pl.when(use_bounded)
    def _():
        bounded_start(0)
        lax.fori_loop(0, n_big, bounded_tile, 0)

    @pl.when(jnp.logical_not(use_bounded))
    def _():
        lax.fori_loop(0, n_main, online_tile, 0)


def _attention(qT, k, vT, lq1, lk1, lq2, lk2, g_col, bound, lam_init):
    B, _, T = qT.shape
    tile, big = ATTN_TILE, 2 * ATTN_TILE
    assert (T - MAIN0) % big == 0
    n_main = (T - MAIN0) // tile
    vec = _const_spec((1, ATTN_HEAD_DIM))
    flag = (bound <= MAX_SCORE_BOUND).astype(jnp.int32).reshape(1)
    return pl.pallas_call(
        functools.partial(_attn_kernel, tile=tile, big=big, n_main=n_main, lam_init=lam_init),
        grid=(B, ATTN_HEADS),
        in_specs=[pl.BlockSpec(memory_space=pltpu.SMEM), _const_spec((1, 1)),
                  pl.BlockSpec((None, V_HEAD_DIM, T), lambda b, h: (b, h, 0)),
                  pl.BlockSpec((None, T, V_HEAD_DIM), lambda b, h: (b, 0, h)),
                  pl.BlockSpec((None, None, V_HEAD_DIM, T), lambda b, h: (b, h, 0, 0)),
                  vec, vec, vec, vec, _const_spec((V_HEAD_DIM, 1))],
        out_specs=pl.BlockSpec((None, T, V_HEAD_DIM), lambda b, h: (b, 0, h)),
        out_shape=jax.ShapeDtypeStruct((B, T, D_ATTN), bf16),
        scratch_shapes=[pltpu.VMEM((V_HEAD_DIM, 2 * big), bf16),
                        pltpu.VMEM((V_HEAD_DIM, 2 * big), f32),
                        pltpu.VMEM((SUBLANE, 2 * big), f32),
                        pltpu.VMEM((1, 2 * tile), f32),
                        pltpu.VMEM((tile, 2 * tile), f32),
                        pltpu.VMEM((tile, 2 * tile), f32),
                        pltpu.VMEM((tile, tile), f32),
                        pltpu.VMEM((tile, 2 * big), bf16),
                        pltpu.VMEM((tile, 2 * big), bf16)],
        compiler_params=pltpu.CompilerParams(
            dimension_semantics=("parallel", "parallel"), vmem_limit_bytes=VMEM_LIMIT),
        name="diff_attn",
    )(flag, bound.reshape(1, 1), qT, k, vT, lq1, lk1, lq2, lk2, g_col)


def _rglru_scan(x, j, cw_ref, cb_ref, wr_ref, wi_ref, brg_ref, big_ref, L_ref,
                xtail_ref, h_ref, ga_ref, gu_ref, hs_ref, *, tc):
    @pl.when(j == 0)
    def _():
        xtail_ref[...] = jnp.zeros((SUBLANE, D_REC), f32)
        h_ref[...] = jnp.zeros((1, D_REC), f32)

    ng = tc // SUBLANE
    x3 = x.reshape(ng, SUBLANE, D_REC)
    prev3 = xtail_ref[...].reshape(1, SUBLANE, D_REC)
    sub3 = lax.broadcasted_iota(jnp.int32, (1, SUBLANE, 1), 1)
    xc3 = cw_ref[CONV_WIDTH - 1:CONV_WIDTH, :] * x3
    for k in range(1, CONV_WIDTH):
        rot = pltpu.roll(jnp.concatenate([prev3, x3], axis=0), k, 1)
        shifted = jnp.where(sub3 < k, rot[0:ng], rot[1:ng + 1])
        xc3 = xc3 + cw_ref[CONV_WIDTH - 1 - k:CONV_WIDTH - k, :] * shifted
    xc = xc3.reshape(tc, D_REC) + cb_ref[...]
    xtail_ref[...] = x[tc - SUBLANE:tc, :]

    xcb = xc.astype(bf16)
    half = D_REC // 2

    def gate_lin(w_ref, b_ref):
        lo = jnp.dot(xcb[:, 0:half], w_ref[0], preferred_element_type=f32)
        hi = jnp.dot(xcb[:, half:D_REC], w_ref[1], preferred_element_type=f32)
        return jnp.concatenate([lo, hi], axis=1) + b_ref[...]

    r = jax.nn.sigmoid(gate_lin(wr_ref, brg_ref))
    i = jax.nn.sigmoid(gate_lin(wi_ref, big_ref))
    log_a = LRU_C * r * jax.nn.log_sigmoid(L_ref[...])
    a = jnp.exp(log_a)
    mult = jnp.sqrt(jnp.maximum(-jnp.tanh(log_a) * (a * a + 1.0), 0.0))
    row = lax.broadcasted_iota(jnp.int32, (tc, 1), 0)
    t_idx = j * tc + row
    mult = jnp.where(t_idx == FRONT_PAD, 1.0, mult)
    u = jnp.where(t_idx >= FRONT_PAD, mult * i * xc, 0.0)

    a3 = a.reshape(ng, SUBLANE, D_REC)
    u3 = u.reshape(ng, SUBLANE, D_REC)
    d = 1
    while d < SUBLANE:
        keep = sub3 >= d
        a_s = jnp.where(keep, pltpu.roll(a3, d, 1), 1.0)
        u_s = jnp.where(keep, pltpu.roll(u3, d, 1), 0.0)
        u3 = a3 * u_s + u3
        a3 = a3 * a_s
        d *= 2
    ga_ref[...] = a3.reshape(tc, D_REC)
    gu_ref[...] = u3.reshape(tc, D_REC)

    def carry_group(gi, h):
        r0 = pl.multiple_of(gi * SUBLANE, SUBLANE)
        hg = gu_ref[pl.ds(r0, SUBLANE), :] + ga_ref[pl.ds(r0, SUBLANE), :] * h
        hs_ref[pl.ds(r0, SUBLANE), :] = hg
        return hg[SUBLANE - 1:SUBLANE, :]

    h_ref[...] = lax.fori_loop(0, tc // SUBLANE, carry_group, h_ref[...], unroll=SCAN_UNROLL)


def _rglru_finish(gate, g_ref, o_ref, hs_ref):
    y = hs_ref[...] * jax.nn.gelu(gate)
    ms = jnp.mean(y * y, axis=-1, keepdims=True)
    o_ref[...] = (y * lax.rsqrt(ms + EPS) * g_ref[...]).astype(bf16)


def _out_ffn_kernel(h_ref, oa_ref, or_ref, wo_ref, g_ref, wgu_ref, wd_ref, out_ref, *,
                    leading_unit_dim):
    if leading_unit_dim:
        h_ref, oa_ref, or_ref = h_ref.at[0], oa_ref.at[0], or_ref.at[0]
    out_ref[...] = (h_ref[...]
                    + jnp.dot(oa_ref[...], wo_ref[0:D_ATTN, :], preferred_element_type=f32)
                    + jnp.dot(or_ref[...], wo_ref[D_ATTN:D_ATTN + D_REC, :],
                              preferred_element_type=f32))
    h1 = out_ref[...]
    ms = jnp.mean(h1 * h1, axis=-1, keepdims=True)
    hn = (h1 * lax.rsqrt(ms + EPS) * g_ref[...]).astype(bf16)
    ffn = None
    for c0 in range(0, D_FF, FFN_CHUNK):
        gt = jnp.dot(hn, wgu_ref[:, c0:c0 + FFN_CHUNK], preferred_element_type=f32)
        up = jnp.dot(hn, wgu_ref[:, D_FF + c0:D_FF + c0 + FFN_CHUNK],
                     preferred_element_type=f32)
        act = (jax.nn.silu(gt) * up).astype(bf16)
        part = jnp.dot(act, wd_ref[c0:c0 + FFN_CHUNK, :], preferred_element_type=f32)
        ffn = part if ffn is None else ffn + part
    out_ref[...] += ffn


def _ffn_weight_specs(layer):
    return [_const_spec((D_ATTN + D_REC, D_MODEL), layer), _const_spec((1, D_MODEL)),
            _const_spec((D_MODEL, 2 * D_FF), layer), _const_spec((D_FF, D_MODEL), layer)]


def _out_ffn(h, oa, orec, wo, g, wgu, wd, layer):
    R, D = h.shape
    tm = FFN_ROW_TILE
    row_blk = lambda width: pl.BlockSpec((tm, width), lambda i: (i, 0))
    return pl.pallas_call(
        functools.partial(_out_ffn_kernel, leading_unit_dim=False),
        grid=(R // tm,),
        in_specs=[row_blk(D), row_blk(D_ATTN), row_blk(D_REC)] + _ffn_weight_specs(layer),
        out_specs=row_blk(D),
        out_shape=jax.ShapeDtypeStruct((R, D), f32),
        compiler_params=pltpu.CompilerParams(
            dimension_semantics=("parallel",), vmem_limit_bytes=VMEM_LIMIT),
        name="out_ffn",
    )(h, oa, orec, wo, g, wgu, wd)


def _out_ffn_last(h, oa, orec, wo, g, wgu, wd, layer):
    B, T, D = h.shape
    S = T - MAIN0
    tm = FFN_ROW_TILE

    def win(width):
        return pl.BlockSpec((pl.Element(1), pl.Element(tm), pl.Element(width)),
                            lambda b, i: (b, pl.multiple_of(MAIN0 + i * tm, LANE), 0))

    return pl.pallas_call(
        functools.partial(_out_ffn_kernel, leading_unit_dim=True),
        grid=(B, S // tm),
        in_specs=[win(D), win(D_ATTN), win(D_REC)] + _ffn_weight_specs(layer),
        out_specs=pl.BlockSpec((None, tm, D), lambda b, i: (b, i, 0)),
        out_shape=jax.ShapeDtypeStruct((B, S, D), f32),
        compiler_params=pltpu.CompilerParams(
            dimension_semantics=("parallel", "parallel"), vmem_limit_bytes=VMEM_LIMIT),
        name="out_ffn_last",
    )(h, oa, orec, wo, g, wgu, wd)


def _rope_tables(T):
    half = ROPE_DIM // 2
    inv = ROPE_THETA ** (-jnp.arange(0, ROPE_DIM, 2, dtype=f32) / ROPE_DIM)
    pos = (jnp.arange(T) - FRONT_PAD).astype(f32)
    ang = pos[:, None] * inv[None, :]
    d = jnp.arange(LANE) % ATTN_HEAD_DIM
    first, second = d < half, (d >= half) & (d < ROPE_DIM)
    cos_l = jnp.cos(ang)[:, d % half]
    sin_l = jnp.sin(ang)[:, d % half]
    cos_t = jnp.where(first | second, cos_l, 1.0)
    sin_a = jnp.where(first, -sin_l, 0.0)
    sin_b = jnp.where(second, sin_l, 0.0)
    return cos_t, sin_a, sin_b


def _block_diag_halves(w):
    nb = REC_BLOCKS // 2
    w4 = w.reshape(2, nb, REC_BLOCK_DIM, REC_BLOCK_DIM)
    eye = jnp.eye(nb, dtype=w.dtype)
    half = D_REC // 2
    return jnp.einsum('hnde,nm->hndme', w4, eye).reshape(2, half, half)


def kernel(x, meta_tokens, norm_mix_g, w_in, q_norm_g, k_norm_g, lambda_q1, lambda_k1,
           lambda_q2, lambda_k2, subln_g, conv_w, conv_b, w_rg, b_rg, w_ig, b_ig, lru_L,
           rec_norm_g, w_out, norm_ffn_g, w_gu, w_down):
    B, S, D = x.shape
    assert D == D_MODEL and S % ATTN_TILE == 0
    T = S + LANE
    assert T % ROW_TILE == 0 and (B * T) % FFN_ROW_TILE == 0

    x_pad = jnp.concatenate([jnp.zeros((FRONT_PAD, D), x.dtype), meta_tokens.astype(x.dtype)], 0)
    cos_t, sin_a, sin_b = _rope_tables(T)
    grp = jnp.arange(D_ATTN // 2) // ATTN_HEAD_DIM
    gmean = jnp.where(grp[:, None] == grp[None, :], 1.0 / ATTN_HEAD_DIM, 0.0).astype(bf16)
    n_grp = D_ATTN // ATTN_HEAD_DIM
    row = lambda v: v.reshape(1, -1)
    w_in_b, w_out_b = w_in.astype(bf16), w_out.astype(bf16)
    w_gu_b, w_down_b = w_gu.astype(bf16), w_down.astype(bf16)

    h = x
    for l in range(DEPTH):
        lam_init = 0.8 - 0.6 * math.exp(-0.3 * l)
        outs = _in_proj(
            h, row(norm_mix_g[l]), w_in_b, l,
            row(jnp.tile(q_norm_g[l], n_grp)), row(jnp.tile(k_norm_g[l], n_grp)),
            cos_t, sin_a, sin_b, gmean,
            (conv_w[l], row(conv_b[l]), _block_diag_halves(w_rg[l]).astype(bf16),
             _block_diag_halves(w_ig[l]).astype(bf16), row(b_rg[l]), row(b_ig[l]),
             row(lru_L[l]), row(rec_norm_g[l])),
            x_pad=x_pad if l == 0 else None)
        qT, k, vT, o_rec = outs[:N_PROJ_OUT]
        if l == 0:
            h = outs[N_PROJ_OUT]
        bound = (ATTN_HEAD_DIM ** 0.5 * math.log2(math.e)
                 * jnp.max(jnp.abs(q_norm_g[l])) * jnp.max(jnp.abs(k_norm_g[l])))
        o_attn = _attention(qT, k, vT, row(lambda_q1[l]), row(lambda_k1[l]),
                            row(lambda_q2[l]), row(lambda_k2[l]),
                            subln_g[l].reshape(V_HEAD_DIM, 1), bound, lam_init)
        ffn_w = (w_out_b, row(norm_ffn_g[l]), w_gu_b, w_down_b, l)
        if l == DEPTH - 1:
            return _out_ffn_last(h, o_attn, o_rec, *ffn_w)
        h = _out_ffn(h.reshape(B * T, D), o_attn.reshape(B * T, D_ATTN),
                     o_rec.reshape(B * T, D_REC), *ffn_w).reshape(B, T, D)
```

```python
import functools
import math

import jax
import jax.numpy as jnp
from jax import lax
from jax.experimental import pallas as pl
from jax.experimental.pallas import tpu as pltpu

D_MODEL = 1024
DEPTH = 2
N_META = 16
D_ATTN = 512
D_REC = 512
ATTN_HEADS = 4
ATTN_HEAD_DIM = 64
V_HEAD_DIM = 2 * ATTN_HEAD_DIM
ROPE_DIM = ATTN_HEAD_DIM // 4
ROPE_THETA = 500000.0
REC_BLOCKS = 8
REC_BLOCK_DIM = D_REC // REC_BLOCKS
CONV_WIDTH = 4
LRU_C = 8.0
D_FF = 2816
EPS = 1e-6
N_IN = 3 * D_ATTN + 2 * D_REC

LANE = 128
SUBLANE = 8
FRONT_PAD = LANE - N_META
MAIN0 = LANE
NEG = -1e30
MAX_SCORE_BOUND = 50.0
VMEM_LIMIT = 56 * 1024 * 1024

ROW_TILE = 640
REC_ROW_TILE = 1664
SCAN_UNROLL = 8
FFN_ROW_TILE = 512
FFN_CHUNK = 256
ATTN_TILE = 512
ATTN_COL_BLOCK = 256

f32 = jnp.float32
bf16 = jnp.bfloat16


def _const_spec(shape, layer=None):
    zeros = (0,) * len(shape)
    if layer is None:
        return pl.BlockSpec(shape, lambda *_: zeros, pipeline_mode=pl.Buffered(1))
    return pl.BlockSpec((None,) + tuple(shape), lambda *_: (layer,) + zeros,
                        pipeline_mode=pl.Buffered(1))


def _qk_norm_rope(z, gain, gmean_ref, cos, sin_a, sin_b, scale):
    zz = (z * z).astype(bf16)
    half = D_ATTN // 2
    ms = jnp.concatenate(
        [jnp.dot(zz[:, 0:half], gmean_ref[...], preferred_element_type=f32),
         jnp.dot(zz[:, half:D_ATTN], gmean_ref[...], preferred_element_type=f32)], axis=1)
    zn = z * lax.rsqrt(ms + EPS) * gain
    if scale != 1.0:
        zn = zn * scale
    outs = []
    for c in range(D_ATTN // LANE):
        zc = zn[:, c * LANE:(c + 1) * LANE]
        half = ROPE_DIM // 2
        outs.append(zc * cos
                    + pltpu.roll(zc, LANE - half, 1) * sin_a
                    + pltpu.roll(zc, half, 1) * sin_b)
    return jnp.concatenate(outs, axis=1)


def _in_proj_first_kernel(x_ref, pad_ref, g_ref, w_ref, qg_ref, kg_ref, cos_ref, sa_ref, sb_ref,
                          gmean_ref, qT_ref, k_ref, vT_ref, xr_ref, gate_ref, h0_ref, *, tm):
    j = pl.program_id(1)

    @pl.when(j == 0)
    def _():
        h0_ref[0:MAIN0, :] = pad_ref[...]
        h0_ref[MAIN0:tm, :] = x_ref[0, 0:tm - MAIN0, :]

    @pl.when(j > 0)
    def _():
        h0_ref[...] = x_ref[0]

    _in_proj_kernel(h0_ref, g_ref, w_ref, qg_ref, kg_ref, cos_ref, sa_ref, sb_ref, gmean_ref,
                    qT_ref, k_ref, vT_ref, xr_ref, gate_ref, tm=tm)


def _in_proj_kernel(h_ref, g_ref, w_ref, qg_ref, kg_ref, cos_ref, sa_ref, sb_ref, gmean_ref,
                    qT_ref, k_ref, vT_ref, xr_ref, gate_ref, *, tm):
    j = pl.program_id(1)
    x = h_ref[...]
    row = j * tm + lax.broadcasted_iota(jnp.int32, (tm, 1), 0)
    ms = jnp.mean(x * x, axis=-1, keepdims=True)
    hn = x * lax.rsqrt(ms + EPS) * g_ref[...]
    hn = jnp.where(row >= FRONT_PAD, hn, 0.0).astype(bf16)

    def proj(c0, width):
        return jnp.dot(hn, w_ref[:, c0:c0 + width], preferred_element_type=f32)

    cos, sa, sb = cos_ref[...], sa_ref[...], sb_ref[...]
    q = _qk_norm_rope(proj(0, D_ATTN), qg_ref[...], gmean_ref, cos, sa, sb,
                      ATTN_HEAD_DIM ** -0.5 * math.log2(math.e))
    qT_ref[...] = q.T.astype(bf16)
    k = _qk_norm_rope(proj(D_ATTN, D_ATTN), kg_ref[...], gmean_ref, cos, sa, sb, 1.0)
    k_ref[...] = k.astype(bf16)
    vT = proj(2 * D_ATTN, D_ATTN).T.astype(bf16)
    for hd in range(ATTN_HEADS):
        vT_ref[hd] = vT[hd * V_HEAD_DIM:(hd + 1) * V_HEAD_DIM, :]
    xr_ref[...] = proj(3 * D_ATTN, D_REC)
    gate_ref[...] = proj(3 * D_ATTN + D_REC, D_REC)


def _in_proj(h, g, w, layer, qg, kg, cos_t, sa_t, sb_t, gmean, x_pad=None):
    first = x_pad is not None
    B, D = h.shape[0], h.shape[2]
    T = h.shape[1] + (MAIN0 if first else 0)
    tm = ROW_TILE
    nt = T // tm
    row_blk = lambda width: pl.BlockSpec((None, tm, width), lambda b, j: (b, j, 0))
    tab_blk = pl.BlockSpec((tm, LANE), lambda b, j: (j, 0))
    if first:
        x_win = pl.BlockSpec(
            (pl.Element(1), pl.Element(tm), pl.Element(D)),
            lambda b, j: (b, pl.multiple_of(jnp.maximum(j * tm - MAIN0, 0), LANE), 0))
        lead_specs, lead_args = [x_win, _const_spec((MAIN0, D))], (h, x_pad)
        body = functools.partial(_in_proj_first_kernel, tm=tm)
    else:
        lead_specs, lead_args = [row_blk(D)], (h,)
        body = functools.partial(_in_proj_kernel, tm=tm)
    out_specs = [pl.BlockSpec((None, D_ATTN, tm), lambda b, j: (b, 0, j)),
                 row_blk(D_ATTN),
                 pl.BlockSpec((None, ATTN_HEADS, V_HEAD_DIM, tm), lambda b, j: (b, 0, 0, j)),
                 row_blk(D_REC), row_blk(D_REC)]
    out_shape = [jax.ShapeDtypeStruct((B, D_ATTN, T), bf16),
                 jax.ShapeDtypeStruct((B, T, D_ATTN), bf16),
                 jax.ShapeDtypeStruct((B, ATTN_HEADS, V_HEAD_DIM, T), bf16),
                 jax.ShapeDtypeStruct((B, T, D_REC), f32),
                 jax.ShapeDtypeStruct((B, T, D_REC), f32)]
    if first:
        out_specs.append(row_blk(D))
        out_shape.append(jax.ShapeDtypeStruct((B, T, D), f32))
    return pl.pallas_call(
        body,
        grid=(B, nt),
        in_specs=lead_specs + [_const_spec((1, D)), _const_spec((D, N_IN), layer),
                               _const_spec((1, D_ATTN)), _const_spec((1, D_ATTN)),
                               tab_blk, tab_blk, tab_blk,
                               _const_spec((D_ATTN // 2, D_ATTN // 2))],
        out_specs=out_specs,
        out_shape=out_shape,
        compiler_params=pltpu.CompilerParams(
            dimension_semantics=("parallel", "parallel"), vmem_limit_bytes=VMEM_LIMIT),
        name="in_proj",
    )(*lead_args, g, w, qg, kg, cos_t, sa_t, sb_t, gmean)


def _attn_kernel(flag_ref, bound_ref, qT_ref, k_ref, vT_ref, lq1_ref, lk1_ref, lq2_ref,
                 lk2_ref, g_ref, o_ref,
                 q2_ref, acc_ref, l_ref, m_ref, sa_ref, sb_ref, bias_ref, pa_ref, pb_ref, *, tile, big,
                 n_main, lam_init):
    lam = (jnp.exp(jnp.sum(lq1_ref[...] * lk1_ref[...], keepdims=True))
           - jnp.exp(jnp.sum(lq2_ref[...] * lk2_ref[...], keepdims=True)) + lam_init)
    hd = ATTN_HEAD_DIM

    def load_q(qs, w):
        qt = qT_ref[:, pl.ds(qs, w)]
        zero = jnp.zeros((hd, w), bf16)
        q2_ref[0:hd, 0:w] = qt[0:hd]
        q2_ref[hd:2 * hd, 0:w] = zero
        q2_ref[0:hd, w:2 * w] = zero
        q2_ref[hd:2 * hd, w:2 * w] = qt[hd:2 * hd]

    def colsum8(p):
        return jnp.sum(p.reshape(p.shape[0] // SUBLANE, SUBLANE, p.shape[1]), axis=0)

    def finish_tile(qs, w):
        denom = jnp.sum(l_ref[:, 0:2 * w], axis=0, keepdims=True)
        on = acc_ref[:, 0:2 * w] * (1.0 / denom)
        o = on[:, 0:w] - lam * on[:, w:2 * w]
        ms = jnp.mean(o * o, axis=0, keepdims=True)
        y = o * lax.rsqrt(ms + EPS) * g_ref[...] * (1.0 - lam_init)
        o_ref[pl.ds(qs, w), :] = y.T.astype(bf16)

    w0 = LANE
    load_q(0, w0)
    s0 = jnp.dot(k_ref[0:w0, :], q2_ref[:, 0:2 * w0], preferred_element_type=f32)
    kidx = lax.broadcasted_iota(jnp.int32, (w0, 2 * w0), 0)
    col = lax.broadcasted_iota(jnp.int32, (w0, 2 * w0), 1)
    qidx = jnp.where(col >= w0, col - w0, col)
    s0 = jnp.where((kidx >= FRONT_PAD) & (kidx <= qidx), s0, NEG)
    p0 = jnp.exp2(s0 - jnp.max(s0, axis=0, keepdims=True))
    l_ref[:, 0:2 * w0] = colsum8(p0)
    acc_ref[:, 0:2 * w0] = jnp.dot(vT_ref[:, 0:w0], p0.astype(bf16), preferred_element_type=f32)
    finish_tile(0, w0)

    kk = lax.broadcasted_iota(jnp.int32, (tile, tile), 0)
    cc = lax.broadcasted_iota(jnp.int32, (tile, tile), 1)
    bias_ref[...] = jnp.where(kk <= cc, 0.0, NEG)

    def tri_bias(cs):
        c0 = cs.start % tile
        return bias_ref[:, c0:c0 + (cs.stop - cs.start)]

    def meta_keys():
        sm = jnp.dot(k_ref[FRONT_PAD:MAIN0, :], q2_ref[:, 0:2 * tile],
                     preferred_element_type=f32)
        m0 = jnp.max(sm, axis=0, keepdims=True)
        pm = jnp.exp2(sm - m0)
        l_ref[:, 0:2 * tile] = colsum8(pm)
        pm = jnp.concatenate([jnp.zeros((FRONT_PAD, 2 * tile), bf16), pm.astype(bf16)], axis=0)
        acc_ref[:, 0:2 * tile] = jnp.dot(vT_ref[:, 0:MAIN0], pm, preferred_element_type=f32)
        m_ref[...] = m0

    def scores(ks, s_ref):
        s_ref[...] = jnp.dot(k_ref[pl.ds(ks, tile), :], q2_ref[:, 0:2 * tile],
                             preferred_element_type=f32)

    def softmax_pv(ks, s_ref, diagonal):
        vc = vT_ref[:, pl.ds(ks, tile)]
        for c0 in range(0, 2 * tile, ATTN_COL_BLOCK):
            cs = slice(c0, c0 + ATTN_COL_BLOCK)
            sc = s_ref[:, cs]
            if diagonal:
                sc = sc + tri_bias(cs)
            m_prev = m_ref[:, cs]
            m_new = jnp.maximum(m_prev, jnp.max(sc, axis=0, keepdims=True))
            alpha = jnp.exp2(m_prev - m_new)
            pc = jnp.exp2(sc - m_new)
            l_ref[:, cs] = alpha * l_ref[:, cs] + colsum8(pc)
            acc_ref[:, cs] = alpha * acc_ref[:, cs] + jnp.dot(vc, pc.astype(bf16),
                                                               preferred_element_type=f32)
            m_ref[:, cs] = m_new

    bound = bound_ref[...]
    n_big = n_main * tile // big

    def sub_blocks(r):
        return [slice(c * big + r * tile + c0, c * big + r * tile + c0 + ATTN_COL_BLOCK)
                for c in range(2) for c0 in range(0, tile, ATTN_COL_BLOCK)]

    all_blocks = sub_blocks(0) + sub_blocks(1)

    def stage_p(kc, cs, p_ref, diagonal):
        sc = jnp.dot(kc, q2_ref[:, cs], preferred_element_type=f32)
        if diagonal:
            sc = sc + tri_bias(cs)
        pc = jnp.exp2(sc - bound)
        l_ref[:, cs] += colsum8(pc)
        p_ref[:, cs] = pc.astype(bf16)

    def stage_pv(vc, cs, p_ref):
        acc_ref[:, cs] += jnp.dot(vc, p_ref[:, cs], preferred_element_type=f32)

    def bounded_start(ti):
        qs = pl.multiple_of(MAIN0 + ti * big, LANE)
        load_q(qs, big)
        sm = jnp.dot(k_ref[FRONT_PAD:MAIN0, :], q2_ref[...], preferred_element_type=f32)
        pm = jnp.exp2(sm - bound)
        l_ref[...] = colsum8(pm)
        pm = jnp.concatenate([jnp.zeros((FRONT_PAD, 2 * big), bf16), pm.astype(bf16)], axis=0)
        acc_ref[...] = jnp.dot(vT_ref[:, 0:MAIN0], pm, preferred_element_type=f32)
        kc = k_ref[pl.ds(qs, tile), :]
        for cs in sub_blocks(0):
            stage_p(kc, cs, pa_ref, True)
        for cs in sub_blocks(1):
            stage_p(kc, cs, pa_ref, False)

    def bounded_tile(ti, carry):
        qs = pl.multiple_of(MAIN0 + ti * big, LANE)

        def full_start(j):
            return pl.multiple_of(MAIN0 + j * tile, LANE)

        def step(v_start, k_start, cur_ref, next_ref):
            vc = vT_ref[:, pl.ds(v_start, tile)]
            kc = k_ref[pl.ds(k_start, tile), :]
            for cs in all_blocks:
                stage_pv(vc, cs, cur_ref)
                stage_p(kc, cs, next_ref, False)

        def pair(t, c):
            held = pl.multiple_of(jnp.where(t == 0, qs, MAIN0 + (2 * t - 1) * tile), LANE)
            step(held, full_start(2 * t), pa_ref, pb_ref)
            step(full_start(2 * t), full_start(2 * t + 1), pb_ref, pa_ref)
            return c

        lax.fori_loop(0, ti, pair, 0)

        held = pl.multiple_of(jnp.where(ti == 0, qs, MAIN0 + (2 * ti - 1) * tile), LANE)
        vc = vT_ref[:, pl.ds(held, tile)]
        kd = k_ref[pl.ds(qs + tile, tile), :]
        for cs in sub_blocks(0):
            stage_pv(vc, cs, pa_ref)
        for cs in sub_blocks(1):
            stage_pv(vc, cs, pa_ref)
            stage_p(kd, cs, pb_ref, True)
        vd = vT_ref[:, pl.ds(qs + tile, tile)]
        for cs in sub_blocks(1):
            stage_pv(vd, cs, pb_ref)
        finish_tile(qs, big)
        bounded_start(jnp.minimum(ti + 1, n_big - 1))
        return carry

    def online_tile(qi, carry):
        qs = pl.multiple_of(MAIN0 + qi * tile, LANE)
        load_q(qs, tile)
        meta_keys()
        scores(MAIN0, sa_ref)

        def step(ks, cur_ref, next_ref):
            scores(pl.multiple_of(ks + tile, LANE), next_ref)
            softmax_pv(ks, cur_ref, False)

        def pair(t, c):
            ks = pl.multiple_of(MAIN0 + 2 * t * tile, LANE)
            step(ks, sa_ref, sb_ref)
            step(pl.multiple_of(ks + tile, LANE), sb_ref, sa_ref)
            return c

        lax.fori_loop(0, qi // 2, pair, 0)
        odd = qi % 2 == 1

        @pl.when(odd)
        def _():
            step(pl.multiple_of(qs - tile, LANE), sa_ref, sb_ref)
            softmax_pv(qs, sb_ref, True)

        @pl.when(jnp.logical_not(odd))
        def _():
            softmax_pv(qs, sa_ref, True)

        finish_tile(qs, tile)
        return carry

    use_bounded = flag_ref[0] == 1

    @pl.when(use_bounded)
    def _():
        bounded_start(0)
        lax.fori_loop(0, n_big, bounded_tile, 0)

    @pl.when(jnp.logical_not(use_bounded))
    def _():
        lax.fori_loop(0, n_main, online_tile, 0)


def _attention(qT, k, vT, lq1, lk1, lq2, lk2, g_col, bound, lam_init):
    B, _, T = qT.shape
    tile, big = ATTN_TILE, 2 * ATTN_TILE
    assert (T - MAIN0) % big == 0
    n_main = (T - MAIN0) // tile
    vec = _const_spec((1, ATTN_HEAD_DIM))
    flag = (bound <= MAX_SCORE_BOUND).astype(jnp.int32).reshape(1)
    return pl.pallas_call(
        functools.partial(_attn_kernel, tile=tile, big=big, n_main=n_main, lam_init=lam_init),
        grid=(B, ATTN_HEADS),
        in_specs=[pl.BlockSpec(memory_space=pltpu.SMEM), _const_spec((1, 1)),
                  pl.BlockSpec((None, V_HEAD_DIM, T), lambda b, h: (b, h, 0)),
                  pl.BlockSpec((None, T, V_HEAD_DIM), lambda b, h: (b, 0, h)),
                  pl.BlockSpec((None, None, V_HEAD_DIM, T), lambda b, h: (b, h, 0, 0)),
                  vec, vec, vec, vec, _const_spec((V_HEAD_DIM, 1))],
        out_specs=pl.BlockSpec((None, T, V_HEAD_DIM), lambda b, h: (b, 0, h)),
        out_shape=jax.ShapeDtypeStruct((B, T, D_ATTN), bf16),
        scratch_shapes=[pltpu.VMEM((V_HEAD_DIM, 2 * big), bf16),
                        pltpu.VMEM((V_HEAD_DIM, 2 * big), f32),
                        pltpu.VMEM((SUBLANE, 2 * big), f32),
                        pltpu.VMEM((1, 2 * tile), f32),
                        pltpu.VMEM((tile, 2 * tile), f32),
                        pltpu.VMEM((tile, 2 * tile), f32),
                        pltpu.VMEM((tile, tile), f32),
                        pltpu.VMEM((tile, 2 * big), bf16),
                        pltpu.VMEM((tile, 2 * big), bf16)],
        compiler_params=pltpu.CompilerParams(
            dimension_semantics=("parallel", "parallel"), vmem_limit_bytes=VMEM_LIMIT),
        name="diff_attn",
    )(flag, bound.reshape(1, 1), qT, k, vT, lq1, lk1, lq2, lk2, g_col)


def _rglru_kernel(xr_ref, gate_ref, cw_ref, cb_ref, wr_ref, wi_ref, brg_ref, big_ref, L_ref,
                  g_ref, o_ref, xtail_ref, h_ref, ga_ref, gu_ref, hs_ref, *, tc):
    j = pl.program_id(1)

    @pl.when(j == 0)
    def _():
        xtail_ref[...] = jnp.zeros((SUBLANE, D_REC), f32)
        h_ref[...] = jnp.zeros((1, D_REC), f32)

    x = xr_ref[...]
    ng = tc // SUBLANE
    x3 = x.reshape(ng, SUBLANE, D_REC)
    prev3 = xtail_ref[...].reshape(1, SUBLANE, D_REC)
    sub3 = lax.broadcasted_iota(jnp.int32, (1, SUBLANE, 1), 1)
    xc3 = cw_ref[CONV_WIDTH - 1:CONV_WIDTH, :] * x3
    for k in range(1, CONV_WIDTH):
        rot = pltpu.roll(jnp.concatenate([prev3, x3], axis=0), k, 1)
        shifted = jnp.where(sub3 < k, rot[0:ng], rot[1:ng + 1])
        xc3 = xc3 + cw_ref[CONV_WIDTH - 1 - k:CONV_WIDTH - k, :] * shifted
    xc = xc3.reshape(tc, D_REC) + cb_ref[...]
    xtail_ref[...] = x[tc - SUBLANE:tc, :]

    xcb = xc.astype(bf16)
    half = D_REC // 2

    def gate_lin(w_ref, b_ref):
        lo = jnp.dot(xcb[:, 0:half], w_ref[0], preferred_element_type=f32)
        hi = jnp.dot(xcb[:, half:D_REC], w_ref[1], preferred_element_type=f32)
        return jnp.concatenate([lo, hi], axis=1) + b_ref[...]

    r = jax.nn.sigmoid(gate_lin(wr_ref, brg_ref))
    i = jax.nn.sigmoid(gate_lin(wi_ref, big_ref))
    log_a = r * (LRU_C * jax.nn.log_sigmoid(L_ref[...]))
    a = jnp.exp(log_a)
    mult = jnp.sqrt(jnp.maximum(-jnp.tanh(log_a) * (a * a + 1.0), 0.0))
    row = lax.broadcasted_iota(jnp.int32, (tc, 1), 0)
    t_idx = j * tc + row
    mult = jnp.where(t_idx == FRONT_PAD, 1.0, mult)
    u = jnp.where(t_idx >= FRONT_PAD, mult * i * xc, 0.0)

    a3 = a.reshape(ng, SUBLANE, D_REC)
    u3 = u.reshape(ng, SUBLANE, D_REC)
    d = 1
    while d < SUBLANE:
        keep = sub3 >= d
        a_s = jnp.where(keep, pltpu.roll(a3, d, 1), 1.0)
        u_s = jnp.where(keep, pltpu.roll(u3, d, 1), 0.0)
        u3 = a3 * u_s + u3
        a3 = a3 * a_s
        d *= 2
    ga_ref[...] = a3.reshape(tc, D_REC)
    gu_ref[...] = u3.reshape(tc, D_REC)

    def carry_group(gi, h):
        r0 = pl.multiple_of(gi * SUBLANE, SUBLANE)
        hg = gu_ref[pl.ds(r0, SUBLANE), :] + ga_ref[pl.ds(r0, SUBLANE), :] * h
        hs_ref[pl.ds(r0, SUBLANE), :] = hg
        return hg[SUBLANE - 1:SUBLANE, :]

    h_ref[...] = lax.fori_loop(0, tc // SUBLANE, carry_group, h_ref[...], unroll=SCAN_UNROLL)
    hseq = hs_ref[...]

    y = hseq * jax.nn.gelu(gate_ref[...])
    ms = jnp.mean(y * y, axis=-1, keepdims=True)
    o_ref[...] = (y * lax.rsqrt(ms + EPS) * g_ref[...]).astype(bf16)


def _rglru(xr, gate, cw, cb, wr, wi, brg, big, L, g):
    B, T, _ = xr.shape
    tc = REC_ROW_TILE
    row_blk = pl.BlockSpec((None, tc, D_REC), lambda b, j: (b, j, 0))
    vec = _const_spec((1, D_REC))
    half = D_REC // 2
    return pl.pallas_call(
        functools.partial(_rglru_kernel, tc=tc),
        grid=(B, T // tc),
        in_specs=[row_blk, row_blk, _const_spec((CONV_WIDTH, D_REC)), vec,
                  _const_spec((2, half, half)), _const_spec((2, half, half)),
                  vec, vec, vec, vec],
        out_specs=row_blk,
        out_shape=jax.ShapeDtypeStruct((B, T, D_REC), bf16),
        scratch_shapes=[pltpu.VMEM((SUBLANE, D_REC), f32),
                        pltpu.VMEM((1, D_REC), f32),
                        pltpu.VMEM((tc, D_REC), f32),
                        pltpu.VMEM((tc, D_REC), f32),
                        pltpu.VMEM((tc, D_REC), f32)],
        compiler_params=pltpu.CompilerParams(
            dimension_semantics=("parallel", "arbitrary"), vmem_limit_bytes=VMEM_LIMIT),
        name="rglru",
    )(xr, gate, cw, cb, wr, wi, brg, big, L, g)


def _out_ffn_kernel(h_ref, oa_ref, or_ref, wo_ref, g_ref, wgu_ref, wd_ref, out_ref, *,
                    leading_unit_dim):
    if leading_unit_dim:
        h_ref, oa_ref, or_ref = h_ref.at[0], oa_ref.at[0], or_ref.at[0]
    out_ref[...] = (h_ref[...]
                    + jnp.dot(oa_ref[...], wo_ref[0:D_ATTN, :], preferred_element_type=f32)
                    + jnp.dot(or_ref[...], wo_ref[D_ATTN:D_ATTN + D_REC, :],
                              preferred_element_type=f32))
    h1 = out_ref[...]
    ms = jnp.mean(h1 * h1, axis=-1, keepdims=True)
    hn = (h1 * lax.rsqrt(ms + EPS) * g_ref[...]).astype(bf16)
    ffn = None
    for c0 in range(0, D_FF, FFN_CHUNK):
        gt = jnp.dot(hn, wgu_ref[:, c0:c0 + FFN_CHUNK], preferred_element_type=f32)
        up = jnp.dot(hn, wgu_ref[:, D_FF + c0:D_FF + c0 + FFN_CHUNK],
                     preferred_element_type=f32)
        act = (jax.nn.silu(gt) * up).astype(bf16)
        part = jnp.dot(act, wd_ref[c0:c0 + FFN_CHUNK, :], preferred_element_type=f32)
        ffn = part if ffn is None else ffn + part
    out_ref[...] += ffn


def _ffn_weight_specs(layer):
    return [_const_spec((D_ATTN + D_REC, D_MODEL), layer), _const_spec((1, D_MODEL)),
            _const_spec((D_MODEL, 2 * D_FF), layer), _const_spec((D_FF, D_MODEL), layer)]


def _out_ffn(h, oa, orec, wo, g, wgu, wd, layer):
    R, D = h.shape
    tm = FFN_ROW_TILE
    row_blk = lambda width: pl.BlockSpec((tm, width), lambda i: (i, 0))
    return pl.pallas_call(
        functools.partial(_out_ffn_kernel, leading_unit_dim=False),
        grid=(R // tm,),
        in_specs=[row_blk(D), row_blk(D_ATTN), row_blk(D_REC)] + _ffn_weight_specs(layer),
        out_specs=row_blk(D),
        out_shape=jax.ShapeDtypeStruct((R, D), f32),
        compiler_params=pltpu.CompilerParams(
            dimension_semantics=("parallel",), vmem_limit_bytes=VMEM_LIMIT),
        name="out_ffn",
    )(h, oa, orec, wo, g, wgu, wd)


def _out_ffn_last(h, oa, orec, wo, g, wgu, wd, layer):
    B, T, D = h.shape
    S = T - MAIN0
    tm = FFN_ROW_TILE

    def win(width):
        return pl.BlockSpec((pl.Element(1), pl.Element(tm), pl.Element(width)),
                            lambda b, i: (b, pl.multiple_of(MAIN0 + i * tm, LANE), 0))

    return pl.pallas_call(
        functools.partial(_out_ffn_kernel, leading_unit_dim=True),
        grid=(B, S // tm),
        in_specs=[win(D), win(D_ATTN), win(D_REC)] + _ffn_weight_specs(layer),
        out_specs=pl.BlockSpec((None, tm, D), lambda b, i: (b, i, 0)),
        out_shape=jax.ShapeDtypeStruct((B, S, D), f32),
        compiler_params=pltpu.CompilerParams(
            dimension_semantics=("parallel", "parallel"), vmem_limit_bytes=VMEM_LIMIT),
        name="out_ffn_last",
    )(h, oa, orec, wo, g, wgu, wd)


def _rope_tables(T):
    half = ROPE_DIM // 2
    inv = ROPE_THETA ** (-jnp.arange(0, ROPE_DIM, 2, dtype=f32) / ROPE_DIM)
    pos = (jnp.arange(T) - FRONT_PAD).astype(f32)
    ang = pos[:, None] * inv[None, :]
    d = jnp.arange(LANE) % ATTN_HEAD_DIM
    first, second = d < half, (d >= half) & (d < ROPE_DIM)
    cos_l = jnp.cos(ang)[:, d % half]
    sin_l = jnp.sin(ang)[:, d % half]
    cos_t = jnp.where(first | second, cos_l, 1.0)
    sin_a = jnp.where(first, -sin_l, 0.0)
    sin_b = jnp.where(second, sin_l, 0.0)
    return cos_t, sin_a, sin_b


def _block_diag_halves(w):
    nb = REC_BLOCKS // 2
    w4 = w.reshape(2, nb, REC_BLOCK_DIM, REC_BLOCK_DIM)
    eye = jnp.eye(nb, dtype=w.dtype)
    half = D_REC // 2
    return jnp.einsum('hnde,nm->hndme', w4, eye).reshape(2, half, half)


def kernel(x, meta_tokens, norm_mix_g, w_in, q_norm_g, k_norm_g, lambda_q1, lambda_k1,
           lambda_q2, lambda_k2, subln_g, conv_w, conv_b, w_rg, b_rg, w_ig, b_ig, lru_L,
           rec_norm_g, w_out, norm_ffn_g, w_gu, w_down):
    B, S, D = x.shape
    assert D == D_MODEL and S % ATTN_TILE == 0
    T = S + LANE
    assert T % ROW_TILE == 0 and T % REC_ROW_TILE == 0 and (B * T) % FFN_ROW_TILE == 0

    x_pad = jnp.concatenate([jnp.zeros((FRONT_PAD, D), x.dtype), meta_tokens.astype(x.dtype)], 0)
    cos_t, sin_a, sin_b = _rope_tables(T)
    grp = jnp.arange(D_ATTN // 2) // ATTN_HEAD_DIM
    gmean = jnp.where(grp[:, None] == grp[None, :], 1.0 / ATTN_HEAD_DIM, 0.0).astype(bf16)
    n_grp = D_ATTN // ATTN_HEAD_DIM
    row = lambda v: v.reshape(1, -1)
    w_in_b, w_out_b = w_in.astype(bf16), w_out.astype(bf16)
    w_gu_b, w_down_b = w_gu.astype(bf16), w_down.astype(bf16)

    h = x
    for l in range(DEPTH):
        lam_init = 0.8 - 0.6 * math.exp(-0.3 * l)
        outs = _in_proj(
            h, row(norm_mix_g[l]), w_in_b, l,
            row(jnp.tile(q_norm_g[l], n_grp)), row(jnp.tile(k_norm_g[l], n_grp)),
            cos_t, sin_a, sin_b, gmean, x_pad=x_pad if l == 0 else None)
        qT, k, vT, xr, gate = outs[:5]
        if l == 0:
            h = outs[5]
        bound = (ATTN_HEAD_DIM ** 0.5 * math.log2(math.e)
                 * jnp.max(jnp.abs(q_norm_g[l])) * jnp.max(jnp.abs(k_norm_g[l])))
        o_attn = _attention(qT, k, vT, row(lambda_q1[l]), row(lambda_k1[l]),
                            row(lambda_q2[l]), row(lambda_k2[l]),
                            subln_g[l].reshape(V_HEAD_DIM, 1), bound, lam_init)
        o_rec = _rglru(xr, gate, conv_w[l], row(conv_b[l]),
                       _block_diag_halves(w_rg[l]).astype(bf16),
                       _block_diag_halves(w_ig[l]).astype(bf16),
                       row(b_rg[l]), row(b_ig[l]), row(lru_L[l]), row(rec_norm_g[l]))
        ffn_w = (w_out_b, row(norm_ffn_g[l]), w_gu_b, w_down_b, l)
        if l == DEPTH - 1:
            return _out_ffn_last(h, o_attn, o_rec, *ffn_w)
        h = _out_ffn(h.reshape(B * T, D), o_attn.reshape(B * T, D_ATTN),
                     o_rec.reshape(B * T, D_REC), *ffn_w).reshape(B, T, D)
```

```python
import functools
import math

import jax
import jax.numpy as jnp
from jax import lax
from jax.experimental import pallas as pl
from jax.experimental.pallas import tpu as pltpu

D_MODEL = 1024
DEPTH = 2
N_META = 16
D_ATTN = 512
D_REC = 512
ATTN_HEADS = 4
ATTN_HEAD_DIM = 64
V_HEAD_DIM = 2 * ATTN_HEAD_DIM
ROPE_DIM = ATTN_HEAD_DIM // 4
ROPE_THETA = 500000.0
REC_BLOCKS = 8
REC_BLOCK_DIM = D_REC // REC_BLOCKS
CONV_WIDTH = 4
LRU_C = 8.0
D_FF = 2816
EPS = 1e-6
N_IN = 3 * D_ATTN + 2 * D_REC

LANE = 128
SUBLANE = 8
FRONT_PAD = LANE - N_META
MAIN0 = LANE
NEG = -1e30
MAX_SCORE_BOUND = 50.0
VMEM_LIMIT = 56 * 1024 * 1024

ROW_TILE = 640
SCAN_UNROLL = 8
FFN_ROW_TILE = 512
FFN_CHUNK = 256
FFN_ROW_GROUPS = 2
ATTN_TILE = 512
ATTN_COL_BLOCK = 256

f32 = jnp.float32
bf16 = jnp.bfloat16


def _const_spec(shape, layer=None):
    zeros = (0,) * len(shape)
    if layer is None:
        return pl.BlockSpec(shape, lambda *_: zeros, pipeline_mode=pl.Buffered(1))
    return pl.BlockSpec((None,) + tuple(shape), lambda *_: (layer,) + zeros,
                        pipeline_mode=pl.Buffered(1))


def _qk_norm_rope(z, gain, gmean_ref, cos, sin_a, sin_b, scale):
    zz = (z * z).astype(bf16)
    half = D_ATTN // 2
    ms = jnp.concatenate(
        [jnp.dot(zz[:, 0:half], gmean_ref[...], preferred_element_type=f32),
         jnp.dot(zz[:, half:D_ATTN], gmean_ref[...], preferred_element_type=f32)], axis=1)
    zn = z * lax.rsqrt(ms + EPS) * gain
    if scale != 1.0:
        zn = zn * scale
    outs = []
    for c in range(D_ATTN // LANE):
        zc = zn[:, c * LANE:(c + 1) * LANE]
        half = ROPE_DIM // 2
        outs.append(zc * cos
                    + pltpu.roll(zc, LANE - half, 1) * sin_a
                    + pltpu.roll(zc, half, 1) * sin_b)
    return jnp.concatenate(outs, axis=1)


def _in_proj_first_kernel(x_ref, pad_ref, g_ref, w_ref, qg_ref, kg_ref, cos_ref, sa_ref, sb_ref,
                          gmean_ref, qT_ref, k_ref, vT_ref, xr_ref, gate_ref, h0_ref, *, tm):
    j = pl.program_id(1)

    @pl.when(j == 0)
    def _():
        h0_ref[0:MAIN0, :] = pad_ref[...]
        h0_ref[MAIN0:tm, :] = x_ref[0, 0:tm - MAIN0, :]

    @pl.when(j > 0)
    def _():
        h0_ref[...] = x_ref[0]

    _in_proj_kernel(h0_ref, g_ref, w_ref, qg_ref, kg_ref, cos_ref, sa_ref, sb_ref, gmean_ref,
                    qT_ref, k_ref, vT_ref, xr_ref, gate_ref, tm=tm)


def _in_proj_kernel(h_ref, g_ref, w_ref, qg_ref, kg_ref, cos_ref, sa_ref, sb_ref, gmean_ref,
                    qT_ref, k_ref, vT_ref, xr_ref, gate_ref, *, tm):
    j = pl.program_id(1)
    x = h_ref[...]
    row = j * tm + lax.broadcasted_iota(jnp.int32, (tm, 1), 0)
    ms = jnp.mean(x * x, axis=-1, keepdims=True)
    hn = x * lax.rsqrt(ms + EPS) * g_ref[...]
    hn = jnp.where(row >= FRONT_PAD, hn, 0.0).astype(bf16)

    def proj(c0, width):
        return jnp.dot(hn, w_ref[:, c0:c0 + width], preferred_element_type=f32)

    cos, sa, sb = cos_ref[...], sa_ref[...], sb_ref[...]
    q = _qk_norm_rope(proj(0, D_ATTN), qg_ref[...], gmean_ref, cos, sa, sb,
                      ATTN_HEAD_DIM ** -0.5 * math.log2(math.e))
    qT_ref[...] = q.T.astype(bf16)
    k = _qk_norm_rope(proj(D_ATTN, D_ATTN), kg_ref[...], gmean_ref, cos, sa, sb, 1.0)
    k_ref[...] = k.astype(bf16)
    vT = proj(2 * D_ATTN, D_ATTN).T.astype(bf16)
    for hd in range(ATTN_HEADS):
        vT_ref[hd] = vT[hd * V_HEAD_DIM:(hd + 1) * V_HEAD_DIM, :]
    xr_ref[...] = proj(3 * D_ATTN, D_REC)
    gate_ref[...] = proj(3 * D_ATTN + D_REC, D_REC)


def _in_proj(h, g, w, layer, qg, kg, cos_t, sa_t, sb_t, gmean, x_pad=None):
    first = x_pad is not None
    B, D = h.shape[0], h.shape[2]
    T = h.shape[1] + (MAIN0 if first else 0)
    tm = ROW_TILE
    nt = T // tm
    row_blk = lambda width: pl.BlockSpec((None, tm, width), lambda b, j: (b, j, 0))
    tab_blk = pl.BlockSpec((tm, LANE), lambda b, j: (j, 0))
    if first:
        x_win = pl.BlockSpec(
            (pl.Element(1), pl.Element(tm), pl.Element(D)),
            lambda b, j: (b, pl.multiple_of(jnp.maximum(j * tm - MAIN0, 0), LANE), 0))
        lead_specs, lead_args = [x_win, _const_spec((MAIN0, D))], (h, x_pad)
        body = functools.partial(_in_proj_first_kernel, tm=tm)
    else:
        lead_specs, lead_args = [row_blk(D)], (h,)
        body = functools.partial(_in_proj_kernel, tm=tm)
    out_specs = [pl.BlockSpec((None, D_ATTN, tm), lambda b, j: (b, 0, j)),
                 row_blk(D_ATTN),
                 pl.BlockSpec((None, ATTN_HEADS, V_HEAD_DIM, tm), lambda b, j: (b, 0, 0, j)),
                 row_blk(D_REC), row_blk(D_REC)]
    out_shape = [jax.ShapeDtypeStruct((B, D_ATTN, T), bf16),
                 jax.ShapeDtypeStruct((B, T, D_ATTN), bf16),
                 jax.ShapeDtypeStruct((B, ATTN_HEADS, V_HEAD_DIM, T), bf16),
                 jax.ShapeDtypeStruct((B, T, D_REC), f32),
                 jax.ShapeDtypeStruct((B, T, D_REC), f32)]
    if first:
        out_specs.append(row_blk(D))
        out_shape.append(jax.ShapeDtypeStruct((B, T, D), f32))
    return pl.pallas_call(
        body,
        grid=(B, nt),
        in_specs=lead_specs + [_const_spec((1, D)), _const_spec((D, N_IN), layer),
                               _const_spec((1, D_ATTN)), _const_spec((1, D_ATTN)),
                               tab_blk, tab_blk, tab_blk,
                               _const_spec((D_ATTN // 2, D_ATTN // 2))],
        out_specs=out_specs,
        out_shape=out_shape,
        compiler_params=pltpu.CompilerParams(
            dimension_semantics=("parallel", "parallel"), vmem_limit_bytes=VMEM_LIMIT),
        name="in_proj",
    )(*lead_args, g, w, qg, kg, cos_t, sa_t, sb_t, gmean)


def _attn_kernel(flag_ref, bound_ref, qT_ref, k_ref, vT_ref, lq1_ref, lk1_ref, lq2_ref,
                 lk2_ref, g_ref, o_ref,
                 q2_ref, acc_ref, l_ref, m_ref, sa_ref, sb_ref, bias_ref, pa_ref, pb_ref, *, tile, big,
                 n_main, lam_init):
    lam = (jnp.exp(jnp.sum(lq1_ref[...] * lk1_ref[...], keepdims=True))
           - jnp.exp(jnp.sum(lq2_ref[...] * lk2_ref[...], keepdims=True)) + lam_init)
    hd = ATTN_HEAD_DIM

    def load_q(qs, w):
        qt = qT_ref[:, pl.ds(qs, w)]
        zero = jnp.zeros((hd, w), bf16)
        q2_ref[0:hd, 0:w] = qt[0:hd]
        q2_ref[hd:2 * hd, 0:w] = zero
        q2_ref[0:hd, w:2 * w] = zero
        q2_ref[hd:2 * hd, w:2 * w] = qt[hd:2 * hd]

    def colsum8(p):
        return jnp.sum(p.reshape(p.shape[0] // SUBLANE, SUBLANE, p.shape[1]), axis=0)

    def finish_tile(qs, w):
        denom = jnp.sum(l_ref[:, 0:2 * w], axis=0, keepdims=True)
        on = acc_ref[:, 0:2 * w] * (1.0 / denom)
        o = on[:, 0:w] - lam * on[:, w:2 * w]
        ms = jnp.mean(o * o, axis=0, keepdims=True)
        y = o * lax.rsqrt(ms + EPS) * g_ref[...] * (1.0 - lam_init)
        o_ref[pl.ds(qs, w), :] = y.T.astype(bf16)

    w0 = LANE
    load_q(0, w0)
    s0 = jnp.dot(k_ref[0:w0, :], q2_ref[:, 0:2 * w0], preferred_element_type=f32)
    kidx = lax.broadcasted_iota(jnp.int32, (w0, 2 * w0), 0)
    col = lax.broadcasted_iota(jnp.int32, (w0, 2 * w0), 1)
    qidx = jnp.where(col >= w0, col - w0, col)
    s0 = jnp.where((kidx >= FRONT_PAD) & (kidx <= qidx), s0, NEG)
    p0 = jnp.exp2(s0 - jnp.max(s0, axis=0, keepdims=True))
    l_ref[:, 0:2 * w0] = colsum8(p0)
    acc_ref[:, 0:2 * w0] = jnp.dot(vT_ref[:, 0:w0], p0.astype(bf16), preferred_element_type=f32)
    finish_tile(0, w0)

    kk = lax.broadcasted_iota(jnp.int32, (tile, tile), 0)
    cc = lax.broadcasted_iota(jnp.int32, (tile, tile), 1)
    bias_ref[...] = jnp.where(kk <= cc, 0.0, NEG)

    def tri_bias(cs):
        c0 = cs.start % tile
        return bias_ref[:, c0:c0 + (cs.stop - cs.start)]

    def meta_keys():
        sm = jnp.dot(k_ref[FRONT_PAD:MAIN0, :], q2_ref[:, 0:2 * tile],
                     preferred_element_type=f32)
        m0 = jnp.max(sm, axis=0, keepdims=True)
        pm = jnp.exp2(sm - m0)
        l_ref[:, 0:2 * tile] = colsum8(pm)
        pm = jnp.concatenate([jnp.zeros((FRONT_PAD, 2 * tile), bf16), pm.astype(bf16)], axis=0)
        acc_ref[:, 0:2 * tile] = jnp.dot(vT_ref[:, 0:MAIN0], pm, preferred_element_type=f32)
        m_ref[...] = m0

    def scores(ks, s_ref):
        s_ref[...] = jnp.dot(k_ref[pl.ds(ks, tile), :], q2_ref[:, 0:2 * tile],
                             preferred_element_type=f32)

    def softmax_pv(ks, s_ref, diagonal):
        vc = vT_ref[:, pl.ds(ks, tile)]
        for c0 in range(0, 2 * tile, ATTN_COL_BLOCK):
            cs = slice(c0, c0 + ATTN_COL_BLOCK)
            sc = s_ref[:, cs]
            if diagonal:
                sc = sc + tri_bias(cs)
            m_prev = m_ref[:, cs]
            m_new = jnp.maximum(m_prev, jnp.max(sc, axis=0, keepdims=True))
            alpha = jnp.exp2(m_prev - m_new)
            pc = jnp.exp2(sc - m_new)
            l_ref[:, cs] = alpha * l_ref[:, cs] + colsum8(pc)
            acc_ref[:, cs] = alpha * acc_ref[:, cs] + jnp.dot(vc, pc.astype(bf16),
                                                               preferred_element_type=f32)
            m_ref[:, cs] = m_new

    bound = bound_ref[...]
    n_big = n_main * tile // big

    def sub_blocks(r):
        return [slice(c * big + r * tile + c0, c * big + r * tile + c0 + ATTN_COL_BLOCK)
                for c in range(2) for c0 in range(0, tile, ATTN_COL_BLOCK)]

    all_blocks = sub_blocks(0) + sub_blocks(1)

    def stage_p(kc, cs, p_ref, diagonal):
        sc = jnp.dot(kc, q2_ref[:, cs], preferred_element_type=f32)
        if diagonal:
            sc = sc + tri_bias(cs)
        pc = jnp.exp2(sc - bound)
        l_ref[:, cs] += colsum8(pc)
        p_ref[:, cs] = pc.astype(bf16)

    def stage_pv(vc, cs, p_ref):
        acc_ref[:, cs] += jnp.dot(vc, p_ref[:, cs], preferred_element_type=f32)

    def bounded_start(ti):
        qs = pl.multiple_of(MAIN0 + ti * big, LANE)
        load_q(qs, big)
        sm = jnp.dot(k_ref[FRONT_PAD:MAIN0, :], q2_ref[...], preferred_element_type=f32)
        pm = jnp.exp2(sm - bound)
        l_ref[...] = colsum8(pm)
        pm = jnp.concatenate([jnp.zeros((FRONT_PAD, 2 * big), bf16), pm.astype(bf16)], axis=0)
        acc_ref[...] = jnp.dot(vT_ref[:, 0:MAIN0], pm, preferred_element_type=f32)
        kc = k_ref[pl.ds(qs, tile), :]
        for cs in sub_blocks(0):
            stage_p(kc, cs, pa_ref, True)
        for cs in sub_blocks(1):
            stage_p(kc, cs, pa_ref, False)

    def bounded_tile(ti, carry):
        qs = pl.multiple_of(MAIN0 + ti * big, LANE)

        def full_start(j):
            return pl.multiple_of(MAIN0 + j * tile, LANE)

        def step(v_start, k_start, cur_ref, next_ref):
            vc = vT_ref[:, pl.ds(v_start, tile)]
            kc = k_ref[pl.ds(k_start, tile), :]
            for cs in all_blocks:
                stage_pv(vc, cs, cur_ref)
                stage_p(kc, cs, next_ref, False)

        def pair(t, c):
            held = pl.multiple_of(jnp.where(t == 0, qs, MAIN0 + (2 * t - 1) * tile), LANE)
            step(held, full_start(2 * t), pa_ref, pb_ref)
            step(full_start(2 * t), full_start(2 * t + 1), pb_ref, pa_ref)
            return c

        lax.fori_loop(0, ti, pair, 0)

        held = pl.multiple_of(jnp.where(ti == 0, qs, MAIN0 + (2 * ti - 1) * tile), LANE)
        vc = vT_ref[:, pl.ds(held, tile)]
        kd = k_ref[pl.ds(qs + tile, tile), :]
        for cs in sub_blocks(0):
            stage_pv(vc, cs, pa_ref)
        for cs in sub_blocks(1):
            stage_pv(vc, cs, pa_ref)
            stage_p(kd, cs, pb_ref, True)
        vd = vT_ref[:, pl.ds(qs + tile, tile)]
        for cs in sub_blocks(1):
            stage_pv(vd, cs, pb_ref)
        finish_tile(qs, big)
        bounded_start(jnp.minimum(ti + 1, n_big - 1))
        return carry

    def online_tile(qi, carry):
        qs = pl.multiple_of(MAIN0 + qi * tile, LANE)
        load_q(qs, tile)
        meta_keys()
        scores(MAIN0, sa_ref)

        def step(ks, cur_ref, next_ref):
            scores(pl.multiple_of(ks + tile, LANE), next_ref)
            softmax_pv(ks, cur_ref, False)

        def pair(t, c):
            ks = pl.multiple_of(MAIN0 + 2 * t * tile, LANE)
            step(ks, sa_ref, sb_ref)
            step(pl.multiple_of(ks + tile, LANE), sb_ref, sa_ref)
            return c

        lax.fori_loop(0, qi // 2, pair, 0)
        odd = qi % 2 == 1

        @pl.when(odd)
        def _():
            step(pl.multiple_of(qs - tile, LANE), sa_ref, sb_ref)
            softmax_pv(qs, sb_ref, True)

        @pl.when(jnp.logical_not(odd))
        def _():
            softmax_pv(qs, sa_ref, True)

        finish_tile(qs, tile)
        return carry

    use_bounded = flag_ref[0] == 1

    @pl.when(use_bounded)
    def _():
        bounded_start(0)
        lax.fori_loop(0, n_big, bounded_tile, 0)

    @pl.when(jnp.logical_not(use_bounded))
    def _():
        lax.fori_loop(0, n_main, online_tile, 0)


def _attention(qT, k, vT, lq1, lk1, lq2, lk2, g_col, bound, lam_init):
    B, _, T = qT.shape
    tile, big = ATTN_TILE, 2 * ATTN_TILE
    assert (T - MAIN0) % big == 0
    n_main = (T - MAIN0) // tile
    vec = _const_spec((1, ATTN_HEAD_DIM))
    flag = (bound <= MAX_SCORE_BOUND).astype(jnp.int32).reshape(1)
    return pl.pallas_call(
        functools.partial(_attn_kernel, tile=tile, big=big, n_main=n_main, lam_init=lam_init),
        grid=(B, ATTN_HEADS),
        in_specs=[pl.BlockSpec(memory_space=pltpu.SMEM), _const_spec((1, 1)),
                  pl.BlockSpec((None, V_HEAD_DIM, T), lambda b, h: (b, h, 0)),
                  pl.BlockSpec((None, T, V_HEAD_DIM), lambda b, h: (b, 0, h)),
                  pl.BlockSpec((None, None, V_HEAD_DIM, T), lambda b, h: (b, h, 0, 0)),
                  vec, vec, vec, vec, _const_spec((V_HEAD_DIM, 1))],
        out_specs=pl.BlockSpec((None, T, V_HEAD_DIM), lambda b, h: (b, 0, h)),
        out_shape=jax.ShapeDtypeStruct((B, T, D_ATTN), bf16),
        scratch_shapes=[pltpu.VMEM((V_HEAD_DIM, 2 * big), bf16),
                        pltpu.VMEM((V_HEAD_DIM, 2 * big), f32),
                        pltpu.VMEM((SUBLANE, 2 * big), f32),
                        pltpu.VMEM((1, 2 * tile), f32),
                        pltpu.VMEM((tile, 2 * tile), f32),
                        pltpu.VMEM((tile, 2 * tile), f32),
                        pltpu.VMEM((tile, tile), f32),
                        pltpu.VMEM((tile, 2 * big), bf16),
                        pltpu.VMEM((tile, 2 * big), bf16)],
        compiler_params=pltpu.CompilerParams(
            dimension_semantics=("parallel", "parallel"), vmem_limit_bytes=VMEM_LIMIT),
        name="diff_attn",
    )(flag, bound.reshape(1, 1), qT, k, vT, lq1, lk1, lq2, lk2, g_col)


def _rglru_kernel(xr_ref, gate_ref, cw_ref, cb_ref, wr_ref, wi_ref, brg_ref, big_ref, L_ref,
                  g_ref, o_ref, xtail_ref, h_ref, ga_ref, gu_ref, hs_ref, *, tc):
    j = pl.program_id(1)

    @pl.when(j == 0)
    def _():
        xtail_ref[...] = jnp.zeros((SUBLANE, D_REC), f32)
        h_ref[...] = jnp.zeros((1, D_REC), f32)

    x = xr_ref[...]
    ng = tc // SUBLANE
    x3 = x.reshape(ng, SUBLANE, D_REC)
    prev3 = xtail_ref[...].reshape(1, SUBLANE, D_REC)
    sub3 = lax.broadcasted_iota(jnp.int32, (1, SUBLANE, 1), 1)
    xc3 = cw_ref[CONV_WIDTH - 1:CONV_WIDTH, :] * x3
    for k in range(1, CONV_WIDTH):
        rot = pltpu.roll(jnp.concatenate([prev3, x3], axis=0), k, 1)
        shifted = jnp.where(sub3 < k, rot[0:ng], rot[1:ng + 1])
        xc3 = xc3 + cw_ref[CONV_WIDTH - 1 - k:CONV_WIDTH - k, :] * shifted
    xc = xc3.reshape(tc, D_REC) + cb_ref[...]
    xtail_ref[...] = x[tc - SUBLANE:tc, :]

    xcb = xc.astype(bf16)
    half = D_REC // 2

    def gate_lin(w_ref, b_ref):
        lo = jnp.dot(xcb[:, 0:half], w_ref[0], preferred_element_type=f32)
        hi = jnp.dot(xcb[:, half:D_REC], w_ref[1], preferred_element_type=f32)
        return jnp.concatenate([lo, hi], axis=1) + b_ref[...]

    r = jax.nn.sigmoid(gate_lin(wr_ref, brg_ref))
    i = jax.nn.sigmoid(gate_lin(wi_ref, big_ref))
    log_a = LRU_C * r * jax.nn.log_sigmoid(L_ref[...])
    a = jnp.exp(log_a)
    mult = jnp.sqrt(jnp.maximum(-jnp.tanh(log_a) * (a * a + 1.0), 0.0))
    row = lax.broadcasted_iota(jnp.int32, (tc, 1), 0)
    t_idx = j * tc + row
    mult = jnp.where(t_idx == FRONT_PAD, 1.0, mult)
    u = jnp.where(t_idx >= FRONT_PAD, mult * i * xc, 0.0)

    a3 = a.reshape(ng, SUBLANE, D_REC)
    u3 = u.reshape(ng, SUBLANE, D_REC)
    d = 1
    while d < SUBLANE:
        keep = sub3 >= d
        a_s = jnp.where(keep, pltpu.roll(a3, d, 1), 1.0)
        u_s = jnp.where(keep, pltpu.roll(u3, d, 1), 0.0)
        u3 = a3 * u_s + u3
        a3 = a3 * a_s
        d *= 2
    ga_ref[...] = a3.reshape(tc, D_REC)
    gu_ref[...] = u3.reshape(tc, D_REC)

    def carry_group(gi, h):
        r0 = pl.multiple_of(gi * SUBLANE, SUBLANE)
        hg = gu_ref[pl.ds(r0, SUBLANE), :] + ga_ref[pl.ds(r0, SUBLANE), :] * h
        hs_ref[pl.ds(r0, SUBLANE), :] = hg
        return hg[SUBLANE - 1:SUBLANE, :]

    h_ref[...] = lax.fori_loop(0, tc // SUBLANE, carry_group, h_ref[...], unroll=SCAN_UNROLL)
    hseq = hs_ref[...]

    y = hseq * jax.nn.gelu(gate_ref[...])
    ms = jnp.mean(y * y, axis=-1, keepdims=True)
    o_ref[...] = (y * lax.rsqrt(ms + EPS) * g_ref[...]).astype(bf16)


def _rglru(xr, gate, cw, cb, wr, wi, brg, big, L, g):
    B, T, _ = xr.shape
    tc = ROW_TILE
    row_blk = pl.BlockSpec((None, tc, D_REC), lambda b, j: (b, j, 0))
    vec = _const_spec((1, D_REC))
    half = D_REC // 2
    return pl.pallas_call(
        functools.partial(_rglru_kernel, tc=tc),
        grid=(B, T // tc),
        in_specs=[row_blk, row_blk, _const_spec((CONV_WIDTH, D_REC)), vec,
                  _const_spec((2, half, half)), _const_spec((2, half, half)),
                  vec, vec, vec, vec],
        out_specs=row_blk,
        out_shape=jax.ShapeDtypeStruct((B, T, D_REC), bf16),
        scratch_shapes=[pltpu.VMEM((SUBLANE, D_REC), f32),
                        pltpu.VMEM((1, D_REC), f32),
                        pltpu.VMEM((tc, D_REC), f32),
                        pltpu.VMEM((tc, D_REC), f32),
                        pltpu.VMEM((tc, D_REC), f32)],
        compiler_params=pltpu.CompilerParams(
            dimension_semantics=("parallel", "arbitrary"), vmem_limit_bytes=VMEM_LIMIT),
        name="rglru",
    )(xr, gate, cw, cb, wr, wi, brg, big, L, g)


def _out_ffn_kernel(h_ref, oa_ref, or_ref, wo_ref, g_ref, wgu_ref, wd_ref, out_ref, *,
                    leading_unit_dim):
    if leading_unit_dim:
        h_ref, oa_ref, or_ref = h_ref.at[0], oa_ref.at[0], or_ref.at[0]
    tm = out_ref.shape[0]
    groups = [slice(r0, r0 + tm // FFN_ROW_GROUPS) for r0 in range(0, tm, tm // FFN_ROW_GROUPS)]
    hns = []
    for rs in groups:
        out_ref[rs, :] = (h_ref[rs, :]
                          + jnp.dot(oa_ref[rs, :], wo_ref[0:D_ATTN, :],
                                    preferred_element_type=f32)
                          + jnp.dot(or_ref[rs, :], wo_ref[D_ATTN:D_ATTN + D_REC, :],
                                    preferred_element_type=f32))
        h1 = out_ref[rs, :]
        ms = jnp.mean(h1 * h1, axis=-1, keepdims=True)
        hns.append((h1 * lax.rsqrt(ms + EPS) * g_ref[...]).astype(bf16))
    ffn = [None] * len(groups)
    for c0 in range(0, D_FF, FFN_CHUNK):
        for gi, hn in enumerate(hns):
            gt = jnp.dot(hn, wgu_ref[:, c0:c0 + FFN_CHUNK], preferred_element_type=f32)
            up = jnp.dot(hn, wgu_ref[:, D_FF + c0:D_FF + c0 + FFN_CHUNK],
                         preferred_element_type=f32)
            act = (jax.nn.silu(gt) * up).astype(bf16)
            part = jnp.dot(act, wd_ref[c0:c0 + FFN_CHUNK, :], preferred_element_type=f32)
            ffn[gi] = part if ffn[gi] is None else ffn[gi] + part
    for rs, f in zip(groups, ffn):
        out_ref[rs, :] += f


def _ffn_weight_specs(layer):
    return [_const_spec((D_ATTN + D_REC, D_MODEL), layer), _const_spec((1, D_MODEL)),
            _const_spec((D_MODEL, 2 * D_FF), layer), _const_spec((D_FF, D_MODEL), layer)]


def _out_ffn(h, oa, orec, wo, g, wgu, wd, layer):
    R, D = h.shape
    tm = FFN_ROW_TILE
    row_blk = lambda width: pl.BlockSpec((tm, width), lambda i: (i, 0))
    return pl.pallas_call(
        functools.partial(_out_ffn_kernel, leading_unit_dim=False),
        grid=(R // tm,),
        in_specs=[row_blk(D), row_blk(D_ATTN), row_blk(D_REC)] + _ffn_weight_specs(layer),
        out_specs=row_blk(D),
        out_shape=jax.ShapeDtypeStruct((R, D), f32),
        compiler_params=pltpu.CompilerParams(
            dimension_semantics=("parallel",), vmem_limit_bytes=VMEM_LIMIT),
        name="out_ffn",
    )(h, oa, orec, wo, g, wgu, wd)


def _out_ffn_last(h, oa, orec, wo, g, wgu, wd, layer):
    B, T, D = h.shape
    S = T - MAIN0
    tm = FFN_ROW_TILE

    def win(width):
        return pl.BlockSpec((pl.Element(1), pl.Element(tm), pl.Element(width)),
                            lambda b, i: (b, pl.multiple_of(MAIN0 + i * tm, LANE), 0))

    return pl.pallas_call(
        functools.partial(_out_ffn_kernel, leading_unit_dim=True),
        grid=(B, S // tm),
        in_specs=[win(D), win(D_ATTN), win(D_REC)] + _ffn_weight_specs(layer),
        out_specs=pl.BlockSpec((None, tm, D), lambda b, i: (b, i, 0)),
        out_shape=jax.ShapeDtypeStruct((B, S, D), f32),
        compiler_params=pltpu.CompilerParams(
            dimension_semantics=("parallel", "parallel"), vmem_limit_bytes=VMEM_LIMIT),
        name="out_ffn_last",
    )(h, oa, orec, wo, g, wgu, wd)


def _rope_tables(T):
    half = ROPE_DIM // 2
    inv = ROPE_THETA ** (-jnp.arange(0, ROPE_DIM, 2, dtype=f32) / ROPE_DIM)
    pos = (jnp.arange(T) - FRONT_PAD).astype(f32)
    ang = pos[:, None] * inv[None, :]
    d = jnp.arange(LANE) % ATTN_HEAD_DIM
    first, second = d < half, (d >= half) & (d < ROPE_DIM)
    cos_l = jnp.cos(ang)[:, d % half]
    sin_l = jnp.sin(ang)[:, d % half]
    cos_t = jnp.where(first | second, cos_l, 1.0)
    sin_a = jnp.where(first, -sin_l, 0.0)
    sin_b = jnp.where(second, sin_l, 0.0)
    return cos_t, sin_a, sin_b


def _block_diag_halves(w):
    nb = REC_BLOCKS // 2
    w4 = w.reshape(2, nb, REC_BLOCK_DIM, REC_BLOCK_DIM)
    eye = jnp.eye(nb, dtype=w.dtype)
    half = D_REC // 2
    return jnp.einsum('hnde,nm->hndme', w4, eye).reshape(2, half, half)


def kernel(x, meta_tokens, norm_mix_g, w_in, q_norm_g, k_norm_g, lambda_q1, lambda_k1,
           lambda_q2, lambda_k2, subln_g, conv_w, conv_b, w_rg, b_rg, w_ig, b_ig, lru_L,
           rec_norm_g, w_out, norm_ffn_g, w_gu, w_down):
    B, S, D = x.shape
    assert D == D_MODEL and S % ATTN_TILE == 0
    T = S + LANE
    assert T % ROW_TILE == 0 and (B * T) % FFN_ROW_TILE == 0

    x_pad = jnp.concatenate([jnp.zeros((FRONT_PAD, D), x.dtype), meta_tokens.astype(x.dtype)], 0)
    cos_t, sin_a, sin_b = _rope_tables(T)
    grp = jnp.arange(D_ATTN // 2) // ATTN_HEAD_DIM
    gmean = jnp.where(grp[:, None] == grp[None, :], 1.0 / ATTN_HEAD_DIM, 0.0).astype(bf16)
    n_grp = D_ATTN // ATTN_HEAD_DIM
    row = lambda v: v.reshape(1, -1)
    w_in_b, w_out_b = w_in.astype(bf16), w_out.astype(bf16)
    w_gu_b, w_down_b = w_gu.astype(bf16), w_down.astype(bf16)

    h = x
    for l in range(DEPTH):
        lam_init = 0.8 - 0.6 * math.exp(-0.3 * l)
        outs = _in_proj(
            h, row(norm_mix_g[l]), w_in_b, l,
            row(jnp.tile(q_norm_g[l], n_grp)), row(jnp.tile(k_norm_g[l], n_grp)),
            cos_t, sin_a, sin_b, gmean, x_pad=x_pad if l == 0 else None)
        qT, k, vT, xr, gate = outs[:5]
        if l == 0:
            h = outs[5]
        bound = (ATTN_HEAD_DIM ** 0.5 * math.log2(math.e)
                 * jnp.max(jnp.abs(q_norm_g[l])) * jnp.max(jnp.abs(k_norm_g[l])))
        o_attn = _attention(qT, k, vT, row(lambda_q1[l]), row(lambda_k1[l]),
                            row(lambda_q2[l]), row(lambda_k2[l]),
                            subln_g[l].reshape(V_HEAD_DIM, 1), bound, lam_init)
        o_rec = _rglru(xr, gate, conv_w[l], row(conv_b[l]),
                       _block_diag_halves(w_rg[l]).astype(bf16),
                       _block_diag_halves(w_ig[l]).astype(bf16),
                       row(b_rg[l]), row(b_ig[l]), row(lru_L[l]), row(rec_norm_g[l]))
        ffn_w = (w_out_b, row(norm_ffn_g[l]), w_gu_b, w_down_b, l)
        if l == DEPTH - 1:
            return _out_ffn_last(h, o_attn, o_rec, *ffn_w)
        h = _out_ffn(h.reshape(B * T, D), o_attn.reshape(B * T, D_ATTN),
                     o_rec.reshape(B * T, D_REC), *ffn_w).reshape(B, T, D)
```
